```python
import jax, jax.numpy as jnp
from jax import lax
import numpy as np

D_MODEL = 2048
BATCH = 4
SEQ = 4096
DEPTH = 2
DEC_BATCH = 8
DEC_SEQ = 16
PAST_LEN = 2048

CHUNK = 64
H_A = 12
N_A = 64
D_A = H_A * N_A
LORA_W = 64
LORA_A = 64
LORA_G = 128
RWKV_GN_EPS = 64e-5
H_B = 4
DK_B = 64
DV_B = 128
LORA_GK = 16
GK_NORMALIZER = 16.0
H_C = 6
DK_C = 128
DV_C = 128
CONV_W = 4
D_MIX = D_A + H_B * DV_B + H_C * DV_C
A_COLS = 3 * D_A + LORA_W + LORA_A + LORA_G
B_COLS = 2 * H_B * DK_B + 2 * H_B * DV_B + LORA_GK
C_QKV = 2 * H_C * DK_C + H_C * DV_C
C_COLS = C_QKV + H_C * DV_C + 2 * H_C
N_IN = A_COLS + B_COLS + C_COLS
A_SPLITS = (D_A, 2 * D_A, 3 * D_A, 3 * D_A + LORA_W, 3 * D_A + LORA_W + LORA_A)
B_SPLITS = (H_B * DK_B, 2 * H_B * DK_B, 2 * H_B * DK_B + H_B * DV_B, 2 * H_B * DK_B + 2 * H_B * DV_B)
C_SPLITS = (C_QKV, C_QKV + H_C * DV_C, C_QKV + H_C * DV_C + H_C)
N_MEM = 256
H_X = 4
HD_X = D_MODEL // H_X
D_FF = 4 * D_MODEL
EPS = 1e-6

kernel_name = 'hybrid_rwkv7_gla_gdn_streaming_step'


def rmsnorm(x, g, eps=EPS):
    xf = x.astype(jnp.float32)
    y = xf * lax.rsqrt(jnp.mean(xf * xf, axis=-1, keepdims=True) + eps)
    return (y * g.astype(jnp.float32)).astype(x.dtype)


def l2norm(x):
    return x * lax.rsqrt(jnp.sum(x * x, axis=-1, keepdims=True) + EPS)


def to_chunks(t, c):
    b, tl, h, d = t.shape
    return t.reshape(b, tl // c, c, h, d).transpose(1, 0, 3, 2, 4)


def from_chunks(t):
    n, b, h, c, d = t.shape
    return t.transpose(1, 0, 3, 2, 4).reshape(b, n * c, h, d)


def rwkv7_mixer(p, shift_buf, s0, mu, w0, w_w2, a0, w_a2, w_g2, k_k, k_a, r_k, lnx_g, lnx_b):
    f32 = jnp.float32
    b, tl, _ = p.shape
    pf = p.astype(f32)
    prev = jnp.concatenate([shift_buf[:, None, :].astype(f32), pf[:, :-1]], axis=1)
    ps = pf + (prev - pf) * mu.astype(f32)
    r, k, v, xw, xa, xg = jnp.split(ps, A_SPLITS, axis=-1)
    w_log = -jax.nn.softplus(-(w0.astype(f32) + jnp.tanh(xw) @ w_w2.astype(f32))) - 0.5
    decay = jnp.exp(-jnp.exp(w_log))
    a = jax.nn.sigmoid(a0.astype(f32) + xa @ w_a2.astype(f32))
    g = jax.nn.sigmoid(xg) @ w_g2.astype(f32)
    heads = lambda t: t.reshape(b, tl, H_A, N_A)
    kk = l2norm(heads(k * k_k.astype(f32)))
    k = k * (1.0 + (a - 1.0) * k_a.astype(f32))
    r_h, k_h, v_h, w_h, a_h = heads(r), heads(k), heads(v), heads(decay), heads(a)

    def step(s, inp):
        r_t, k_t, v_t, w_t, kk_t, b_t = inp
        sa = jnp.einsum('bhvk,bhk->bhv', s, kk_t)
        s = s * w_t[:, :, None, :] - sa[..., None] * b_t[:, :, None, :] + v_t[..., None] * k_t[:, :, None, :]
        return s, jnp.einsum('bhvk,bhk->bhv', s, r_t)

    seq = tuple(jnp.moveaxis(t, 1, 0) for t in (r_h, k_h, v_h, w_h, kk, kk * a_h))
    s_t, y = lax.scan(step, s0.astype(f32), seq)
    y = jnp.moveaxis(y, 0, 1)
    mean = jnp.mean(y, axis=-1, keepdims=True)
    var = jnp.mean(jnp.square(y - mean), axis=-1, keepdims=True)
    y = ((y - mean) * lax.rsqrt(var + RWKV_GN_EPS)).reshape(b, tl, D_A) * lnx_g.astype(f32) + lnx_b.astype(f32)
    bonus = jnp.sum(r_h * k_h * r_k.astype(f32), axis=-1, keepdims=True) * v_h
    y = (y + bonus.reshape(b, tl, D_A)) * g
    return y, pf[:, -1], s_t


def gla_chunked(q, k, v, glog, s0):
    c = min(CHUNK, q.shape[1])
    causal = jnp.tril(jnp.ones((c, c), bool))[:, :, None]

    def body(s, inp):
        q_c, k_c, v_c, g_c = inp
        gc = jnp.cumsum(g_c, axis=2)
        diff = gc[:, :, :, None, :] - gc[:, :, None, :, :]
        dec = jnp.where(causal, jnp.exp(jnp.where(causal, diff, 0.0)), 0.0)
        att = jnp.einsum('bhid,bhjd,bhijd->bhij', q_c, k_c, dec)
        o = jnp.einsum('bhij,bhjv->bhiv', att, v_c) + jnp.einsum('bhid,bhdv->bhiv', q_c * jnp.exp(gc), s)
        g_last = gc[:, :, -1:, :]
        s = s * jnp.exp(g_last[:, :, 0, :, None]) + jnp.einsum('bhjd,bhjv->bhdv', k_c * jnp.exp(g_last - gc), v_c)
        return s, o

    s_t, o = lax.scan(body, s0, tuple(to_chunks(t, c) for t in (q, k, v, glog)))
    return from_chunks(o), s_t


def gla_mixer(p, s0, w_gk2, b_gk, gla_norm_g):
    f32 = jnp.float32
    b, tl, _ = p.shape
    pf = p.astype(f32)
    q, k, v, g, gk = jnp.split(pf, B_SPLITS, axis=-1)
    q = q.reshape(b, tl, H_B, DK_B) * (DK_B ** -0.5)
    k = k.reshape(b, tl, H_B, DK_B)
    v = v.reshape(b, tl, H_B, DV_B)
    glog = (jax.nn.log_sigmoid(gk @ w_gk2.astype(f32) + b_gk.astype(f32)) / GK_NORMALIZER).reshape(b, tl, H_B, DK_B)
    o, s_t = gla_chunked(q, k, v, glog, s0.astype(f32))
    o = rmsnorm(o, gla_norm_g).reshape(b, tl, H_B * DV_B) * jax.nn.silu(g)
    return o, s_t


def gdn_chunked(q, k, v, g, beta, s0):
    b, tl, h, _ = q.shape
    c = min(CHUNK, tl)
    n = tl // c
    qc, kc, vc = to_chunks(q, c), to_chunks(k, c), to_chunks(v, c)
    gc = jnp.cumsum(g.reshape(b, n, c, h).transpose(1, 0, 3, 2), axis=-1)
    bc = beta.reshape(b, n, c, h).transpose(1, 0, 3, 2)[..., None]
    tril = jnp.tril(jnp.ones((c, c), bool))
    strict = jnp.tril(jnp.ones((c, c), bool), -1)
    diff = gc[..., :, None] - gc[..., None, :]
    dec = jnp.where(tril, jnp.exp(jnp.where(tril, diff, 0.0)), 0.0)
    kb = kc * bc
    vb = vc * bc
    m = jnp.eye(c, dtype=jnp.float32) + jnp.where(strict, jnp.einsum('nbhid,nbhjd->nbhij', kb, kc) * dec, 0.0)
    u = lax.linalg.triangular_solve(m, vb, left_side=True, lower=True, unit_diagonal=True)
    w = lax.linalg.triangular_solve(m, kb * jnp.exp(gc)[..., None], left_side=True, lower=True, unit_diagonal=True)
    att = jnp.einsum('nbhid,nbhjd->nbhij', qc, kc) * dec
    qg = qc * jnp.exp(gc)[..., None]
    kd = kc * jnp.exp(gc[..., -1:] - gc)[..., None]
    gl = jnp.exp(gc[..., -1])

    def body(s, inp):
        u_c, w_c, att_c, qg_c, kd_c, gl_c = inp
        v_new = u_c - jnp.einsum('bhid,bhdv->bhiv', w_c, s)
        o = jnp.einsum('bhid,bhdv->bhiv', qg_c, s) + jnp.einsum('bhij,bhjv->bhiv', att_c, v_new)
        s = s * gl_c[..., None, None] + jnp.einsum('bhjd,bhjv->bhdv', kd_c, v_new)
        return s, o

    s_t, o = lax.scan(body, s0, (u, w, att, qg, kd, gl))
    return from_chunks(o), s_t


def gdn_mixer(p, conv_buf, s0, conv_w, a_log, dt_bias, gdn_norm_g):
    f32 = jnp.float32
    b, tl, _ = p.shape
    pf = p.astype(f32)
    qkv, z, a, bb = jnp.split(pf, C_SPLITS, axis=-1)
    xp = jnp.concatenate([conv_buf.astype(f32), qkv], axis=1)
    conv = lax.conv_general_dilated(xp, conv_w.astype(f32)[:, None, :], window_strides=(1,), padding='VALID',
                                    dimension_numbers=('NWC', 'WIO', 'NWC'), feature_group_count=C_QKV)
    q, k, v = jnp.split(jax.nn.silu(conv), (H_C * DK_C, 2 * H_C * DK_C), axis=-1)
    q = l2norm(q.reshape(b, tl, H_C, DK_C)) * (DK_C ** -0.5)
    k = l2norm(k.reshape(b, tl, H_C, DK_C))
    v = v.reshape(b, tl, H_C, DV_C)
    beta = jax.nn.sigmoid(bb)
    g = -jnp.exp(a_log.astype(f32)) * jax.nn.softplus(a + dt_bias.astype(f32))
    o, s_t = gdn_chunked(q, k, v, g, beta, s0.astype(f32))
    o = rmsnorm(o, gdn_norm_g).reshape(b, tl, H_C * DV_C) * jax.nn.silu(z)
    return o, xp[:, -(CONV_W - 1):], s_t


def mem_kv(mem, g, wk, wv):
    m = rmsnorm(mem, g)
    b = mem.shape[0]
    return (m @ wk).reshape(b, N_MEM, H_X, HD_X), (m @ wv).reshape(b, N_MEM, H_X, HD_X)


def cross_attn(h, mk, mv, wq, wo):
    b, tl, _ = h.shape
    q = (h @ wq).reshape(b, tl, H_X, HD_X)
    s = jnp.einsum('bthd,bmhd->bhtm', q, mk).astype(jnp.float32) * (HD_X ** -0.5)
    pr = jax.nn.softmax(s, axis=-1).astype(h.dtype)
    o = jnp.einsum('bhtm,bmhd->bthd', pr, mv).reshape(b, tl, D_MODEL)
    return o @ wo


def trunk(x, mem_k, mem_v, st_shift, st_rwkv, st_gla, st_conv, st_gdn, W):
    dt = x.dtype
    n_shift, n_rwkv, n_gla, n_conv, n_gdn = [], [], [], [], []
    for l in range(DEPTH):
        h = rmsnorm(x, W['norm1_g'][l])
        p = h @ W['w_in'][l]
        pa, pb, pc = jnp.split(p, (A_COLS, A_COLS + B_COLS), axis=-1)
        ya, sh, sa = rwkv7_mixer(pa, st_shift[l], st_rwkv[l], W['mu_a'][l], W['w0_a'][l], W['w_w2'][l],
                                 W['a0_a'][l], W['w_a2'][l], W['w_g2'][l], W['k_k'][l], W['k_a'][l],
                                 W['r_k'][l], W['lnx_g'][l], W['lnx_b'][l])
        yb, sb = gla_mixer(pb, st_gla[l], W['w_gk2'][l], W['b_gk'][l], W['gla_norm_g'][l])
        yc, cb, sc = gdn_mixer(pc, st_conv[l], st_gdn[l], W['conv_w'][l], W['a_log'][l], W['dt_bias'][l],
                               W['gdn_norm_g'][l])
        mix = jnp.concatenate([ya, yb, yc], axis=-1).astype(dt)
        x = x + mix @ W['w_out'][l]
        x = x + cross_attn(rmsnorm(x, W['normx_g'][l]), mem_k[l], mem_v[l], W['wq_x'][l], W['wo_x'][l])
        u = jax.nn.relu(rmsnorm(x, W['norm3_g'][l]) @ W['w_up'][l])
        x = x + (u * u) @ W['w_down'][l]
        n_shift.append(sh.astype(dt))
        n_rwkv.append(sa.astype(dt))
        n_gla.append(sb.astype(dt))
        n_conv.append(cb.astype(dt))
        n_gdn.append(sc.astype(dt))
    y = rmsnorm(x, W['final_g'])
    return y, jnp.stack(n_shift), jnp.stack(n_rwkv), jnp.stack(n_gla), jnp.stack(n_conv), jnp.stack(n_gdn)


def setup_inputs(seed: int = 0) -> dict:
    key = jax.random.key(seed)
    ks = iter(jax.random.split(key, 64))
    f32 = jnp.float32
    nrm = lambda shape, s=1.0: s * jax.random.normal(next(ks), shape, f32)
    uni = lambda shape, lo, hi: jax.random.uniform(next(ks), shape, f32, lo, hi)
    gain = lambda shape: 1.0 + 0.02 * jax.random.normal(next(ks), shape, f32)
    L = DEPTH
    return {
        'x_prompt': nrm((BATCH, SEQ, D_MODEL)),
        'x_sample': nrm((DEC_BATCH, DEC_SEQ, D_MODEL)),
        'state_rwkv_shift': nrm((L, DEC_BATCH, A_COLS)),
        'state_rwkv': nrm((L, DEC_BATCH, H_A, N_A, N_A), 0.5),
        'state_gla': nrm((L, DEC_BATCH, H_B, DK_B, DV_B), 1.0),
        'state_gdn_conv': nrm((L, DEC_BATCH, CONV_W - 1, C_QKV)),
        'state_gdn': nrm((L, DEC_BATCH, H_C, DK_C, DV_C), 0.1),
        'cache_mem_k': nrm((L, DEC_BATCH, N_MEM, H_X, HD_X)),
        'cache_mem_v': nrm((L, DEC_BATCH, N_MEM, H_X, HD_X)),
        'mem_prompt': nrm((BATCH, N_MEM, D_MODEL)),
        'norm1_g': gain((L, D_MODEL)),
        'w_in': nrm((L, D_MODEL, N_IN), D_MODEL ** -0.5),
        'mu_a': uni((L, A_COLS), 0.0, 1.0),
        'w0_a': uni((L, D_A), -3.0, 1.0),
        'w_w2': nrm((L, LORA_W, D_A), 0.5 * LORA_W ** -0.5),
        'a0_a': nrm((L, D_A), 0.1),
        'w_a2': nrm((L, LORA_A, D_A), 0.5 * LORA_A ** -0.5),
        'w_g2': nrm((L, LORA_G, D_A), LORA_G ** -0.5),
        'k_k': 0.85 + nrm((L, D_A), 0.02),
        'k_a': gain((L, D_A)),
        'r_k': nrm((L, H_A, N_A), 0.1),
        'lnx_g': gain((L, D_A)),
        'lnx_b': nrm((L, D_A), 0.01),
        'w_gk2': nrm((L, LORA_GK, H_B * DK_B), LORA_GK ** -0.5),
        'b_gk': nrm((L, H_B * DK_B), 0.01),
        'gla_norm_g': gain((L, DV_B)),
        'conv_w': nrm((L, CONV_W, C_QKV), CONV_W ** -0.5),
        'a_log': jnp.log(uni((L, H_C), 1.0, 16.0)),
        'dt_bias': jnp.log(jnp.expm1(uni((L, H_C), 0.001, 0.1))),
        'gdn_norm_g': gain((L, DV_C)),
        'w_out': nrm((L, D_MIX, D_MODEL), D_MIX ** -0.5),
        'normx_g': gain((L, D_MODEL)),
        'normm_g': gain((L, D_MODEL)),
        'wq_x': nrm((L, D_MODEL, D_MODEL), D_MODEL ** -0.5),
        'wk_x': nrm((L, D_MODEL, D_MODEL), D_MODEL ** -0.5),
        'wv_x': nrm((L, D_MODEL, D_MODEL), D_MODEL ** -0.5),
        'wo_x': nrm((L, D_MODEL, D_MODEL), D_MODEL ** -0.5),
        'norm3_g': gain((L, D_MODEL)),
        'w_up': nrm((L, D_MODEL, D_FF), D_MODEL ** -0.5),
        'w_down': nrm((L, D_FF, D_MODEL), D_FF ** -0.5),
        'final_g': gain((D_MODEL,)),
    }


def reference(x_prompt, x_sample, state_rwkv_shift, state_rwkv, state_gla, state_gdn_conv, state_gdn,
              cache_mem_k, cache_mem_v, mem_prompt, norm1_g, w_in, mu_a, w0_a, w_w2, a0_a, w_a2, w_g2,
              k_k, k_a, r_k, lnx_g, lnx_b, w_gk2, b_gk, gla_norm_g, conv_w, a_log, dt_bias, gdn_norm_g,
              w_out, normx_g, normm_g, wq_x, wk_x, wv_x, wo_x, norm3_g, w_up, w_down, final_g):
    W = dict(norm1_g=norm1_g, w_in=w_in, mu_a=mu_a, w0_a=w0_a, w_w2=w_w2, a0_a=a0_a, w_a2=w_a2, w_g2=w_g2,
             k_k=k_k, k_a=k_a, r_k=r_k, lnx_g=lnx_g, lnx_b=lnx_b, w_gk2=w_gk2, b_gk=b_gk,
             gla_norm_g=gla_norm_g, conv_w=conv_w, a_log=a_log, dt_bias=dt_bias, gdn_norm_g=gdn_norm_g,
             w_out=w_out, normx_g=normx_g, wq_x=wq_x, wo_x=wo_x, norm3_g=norm3_g, w_up=w_up,
             w_down=w_down, final_g=final_g)
    b = x_prompt.shape[0]
    dt = x_prompt.dtype
    mk_list, mv_list = [], []
    for l in range(DEPTH):
        mk_l, mv_l = mem_kv(mem_prompt, normm_g[l], wk_x[l], wv_x[l])
        mk_list.append(mk_l)
        mv_list.append(mv_l)
    p_mem_k = jnp.stack(mk_list)
    p_mem_v = jnp.stack(mv_list)
    y_prompt, p_shift, p_rwkv, p_gla, p_conv, p_gdn = trunk(
        x_prompt, p_mem_k, p_mem_v,
        jnp.zeros((DEPTH, b, A_COLS), dt), jnp.zeros((DEPTH, b, H_A, N_A, N_A), dt),
        jnp.zeros((DEPTH, b, H_B, DK_B, DV_B), dt), jnp.zeros((DEPTH, b, CONV_W - 1, C_QKV), dt),
        jnp.zeros((DEPTH, b, H_C, DK_C, DV_C), dt), W)
    y_sample, s_shift, s_rwkv, s_gla, s_conv, s_gdn = trunk(
        x_sample, cache_mem_k, cache_mem_v, state_rwkv_shift, state_rwkv, state_gla, state_gdn_conv,
        state_gdn, W)
    return (y_prompt, y_sample, p_shift, p_rwkv, p_gla, p_conv, p_gdn, p_mem_k, p_mem_v,
            s_shift, s_rwkv, s_gla, s_conv, s_gdn)
```

```python
import functools

import jax
import jax.numpy as jnp
from jax import lax
from jax.experimental import pallas as pl
from jax.experimental.pallas import tpu as pltpu

F32 = jnp.float32
BF16 = jnp.bfloat16

EPS = 1e-6
RWKV_GN_EPS = 64e-5
GK_NORMALIZER = 16.0
CHUNK = 64
SUB = 16
LANE = 128
VMEM_LIMIT = 56 * 1024 * 1024


def _cparams(*sem):
    return pltpu.CompilerParams(dimension_semantics=sem, vmem_limit_bytes=VMEM_LIMIT)


def _dot(a, b):
    return jnp.dot(a.astype(BF16), b.astype(BF16), preferred_element_type=F32)


def _dot_nt(a, b):
    return lax.dot_general(a.astype(BF16), b.astype(BF16), (((1,), (1,)), ((), ())),
                           preferred_element_type=F32)


def _dot_tn(a, b):
    return lax.dot_general(a.astype(BF16), b.astype(BF16), (((0,), (0,)), ((), ())),
                           preferred_element_type=F32)


def _split(x, parts):
    out = []
    r = x
    for _ in range(parts):
        p = r.astype(BF16)
        out.append(p)
        r = r - p.astype(F32)
    return out


def _dot_exact_lhs(l_bf, x, parts=3):
    acc = None
    for p in _split(x, parts):
        d = jnp.dot(l_bf, p, preferred_element_type=F32)
        acc = d if acc is None else acc + d
    return acc


def _dot_exact_rhs(x, r_bf, parts=2):
    acc = None
    for p in _split(x, parts):
        d = jnp.dot(p, r_bf, preferred_element_type=F32)
        acc = d if acc is None else acc + d
    return acc


def _rms(x, g):
    ms = jnp.mean(x * x, axis=-1, keepdims=True)
    return x * lax.rsqrt(ms + EPS) * g


def _sigmoid(x):
    return 1.0 / (1.0 + jnp.exp(-x))


def _softplus(x):
    return jnp.maximum(x, 0.0) + jnp.log(1.0 + jnp.exp(-jnp.abs(x)))


def _tri(c, strict):
    r = lax.broadcasted_iota(jnp.int32, (c, c), 0)
    q = lax.broadcasted_iota(jnp.int32, (c, c), 1)
    return (r > q) if strict else (r >= q)


def _unit_lower_solve(l, rhs):
    c = l.shape[0]
    x = -l
    out = rhs + _dot(x, rhs)
    p = 2
    while p < c:
        x = _dot(x, x)
        out = out + _dot(x, out)
        p *= 2
    return out


def _norm_matmul_kernel(x_ref, g_ref, w_ref, o_ref, h_ref):
    @pl.when(pl.program_id(1) == 0)
    def _():
        h_ref[...] = _rms(x_ref[...], g_ref[...]).astype(BF16)

    o_ref[...] = jnp.dot(h_ref[...], w_ref[...], preferred_element_type=F32).astype(o_ref.dtype)


def norm_matmul(x, g, w, tm, tn):
    m, k = x.shape
    n = w.shape[1]
    return pl.pallas_call(
        _norm_matmul_kernel,
        grid=(m // tm, n // tn),
        in_specs=[pl.BlockSpec((tm, k), lambda i, j: (i, 0)),
                  pl.BlockSpec((1, k), lambda i, j: (0, 0)),
                  pl.BlockSpec((k, tn), lambda i, j: (0, j))],
        out_specs=pl.BlockSpec((tm, tn), lambda i, j: (i, j)),
        out_shape=jax.ShapeDtypeStruct((m, n), F32),
        scratch_shapes=[pltpu.VMEM((tm, k), BF16)],
        compiler_params=_cparams("parallel", "arbitrary"),
    )(x, g.reshape(1, k), w)


def _out_proj_kernel(x_ref, ya_ref, yb_ref, yc_ref, wa_ref, wb_ref, wc_ref, o_ref):
    acc = jnp.dot(ya_ref[...], wa_ref[...], preferred_element_type=F32)
    acc += jnp.dot(yb_ref[...], wb_ref[...], preferred_element_type=F32)
    acc += jnp.dot(yc_ref[...], wc_ref[...], preferred_element_type=F32)
    o_ref[...] = x_ref[...] + acc


def out_proj(x, ya, yb, yc, wa, wb, wc, tm):
    m, d = x.shape
    row = lambda a: pl.BlockSpec((tm, a.shape[1]), lambda i: (i, 0))
    full = lambda a: pl.BlockSpec(a.shape, lambda i: (0, 0))
    return pl.pallas_call(
        _out_proj_kernel,
        grid=(m // tm,),
        in_specs=[row(x), row(ya), row(yb), row(yc), full(wa), full(wb), full(wc)],
        out_specs=row(x),
        out_shape=jax.ShapeDtypeStruct((m, d), F32),
        compiler_params=_cparams("parallel"),
    )(x, ya, yb, yc, wa, wb, wc)


def _cross_attn_kernel(x_ref, g_ref, wq_ref, wo_ref, k_ref, v_ref, o_ref, *, heads):
    x = x_ref[0]
    d = x.shape[-1]
    hd = d // heads
    h = _rms(x, g_ref[...]).astype(BF16)
    q = jnp.dot(h, wq_ref[...], preferred_element_type=F32).astype(BF16)
    outs = []
    for i in range(heads):
        hs = slice(i * hd, (i + 1) * hd)
        s = _dot_nt(q[:, hs], k_ref[0, :, hs]) * (hd ** -0.5)
        s = s - jnp.max(s, axis=-1, keepdims=True)
        e = jnp.exp(s)
        pr = e / jnp.sum(e, axis=-1, keepdims=True)
        outs.append(jnp.dot(pr.astype(BF16), v_ref[0, :, hs], preferred_element_type=F32).astype(BF16))
    o = jnp.concatenate(outs, axis=-1)
    o_ref[0] = x + jnp.dot(o, wo_ref[...], preferred_element_type=F32)


def cross_attn(x, g, wq, wo, mk, mv, heads, tq):
    b, t, d = x.shape
    n_mem = mk.shape[1]
    full = lambda a: pl.BlockSpec(a.shape, lambda i, j: (0, 0))
    return pl.pallas_call(
        functools.partial(_cross_attn_kernel, heads=heads),
        grid=(b, t // tq),
        in_specs=[pl.BlockSpec((1, tq, d), lambda i, j: (i, j, 0)),
                  pl.BlockSpec((1, d), lambda i, j: (0, 0)),
                  full(wq), full(wo),
                  pl.BlockSpec((1, n_mem, d), lambda i, j: (i, 0, 0)),
                  pl.BlockSpec((1, n_mem, d), lambda i, j: (i, 0, 0))],
        out_specs=pl.BlockSpec((1, tq, d), lambda i, j: (i, j, 0)),
        out_shape=jax.ShapeDtypeStruct((b, t, d), F32),
        compiler_params=_cparams("parallel", "parallel"),
    )(x, g.reshape(1, d), wq, wo, mk, mv)


def _mlp_kernel(x_ref, g_ref, fg_ref, wu_ref, wd_ref, o_ref, h_ref, acc_ref, *, final_norm):
    f = pl.program_id(1)

    @pl.when(f == 0)
    def _():
        h_ref[...] = _rms(x_ref[...], g_ref[...]).astype(BF16)
        acc_ref[...] = jnp.zeros_like(acc_ref)

    u = jnp.maximum(jnp.dot(h_ref[...], wu_ref[...], preferred_element_type=F32), 0.0)
    acc_ref[...] += jnp.dot((u * u).astype(BF16), wd_ref[...], preferred_element_type=F32)

    @pl.when(f == pl.num_programs(1) - 1)
    def _():
        y = x_ref[...] + acc_ref[...]
        if final_norm:
            y = _rms(y, fg_ref[...])
        o_ref[...] = y


def mlp(x, g, fg, wu, wd, tm, tf, final_norm):
    m, d = x.shape
    dff = wu.shape[1]
    return pl.pallas_call(
        functools.partial(_mlp_kernel, final_norm=final_norm),
        grid=(m // tm, dff // tf),
        in_specs=[pl.BlockSpec((tm, d), lambda i, j: (i, 0)),
                  pl.BlockSpec((1, d), lambda i, j: (0, 0)),
                  pl.BlockSpec((1, d), lambda i, j: (0, 0)),
                  pl.BlockSpec((d, tf), lambda i, j: (0, j)),
                  pl.BlockSpec((tf, d), lambda i, j: (j, 0))],
        out_specs=pl.BlockSpec((tm, d), lambda i, j: (i, 0)),
        out_shape=jax.ShapeDtypeStruct((m, d), F32),
        scratch_shapes=[pltpu.VMEM((tm, d), BF16), pltpu.VMEM((tm, d), F32)],
        compiler_params=_cparams("parallel", "arbitrary"),
    )(x, g.reshape(1, d), fg.reshape(1, d), wu, wd)


def _rwkv_kernel(p_ref, shift0_ref, s0_ref, mu_ref, w0_ref, ww2_ref, a0_ref, wa2_ref, wg2_ref,
                 kk_ref, ka_ref, rk_ref, lng_ref, lnb_ref, seg_ref, segt_ref,
                 y_ref, shift_ref, s_ref,
                 carry_ref, state_ref, fe_ref, yh_ref, *, heads, hd, lora_w, lora_a):
    n = pl.program_id(1)
    c = p_ref.shape[1]
    da = heads * hd

    @pl.when(n == 0)
    def _():
        carry_ref[...] = shift0_ref[0]
        state_ref[...] = s0_ref[0]

    pf = p_ref[0]
    row = lax.broadcasted_iota(jnp.int32, pf.shape, 0)
    prev = jnp.where(row == 0, carry_ref[...], pltpu.roll(pf, shift=1, axis=0))
    carry_ref[...] = pf[c - 1:c, :]
    ps = pf + (prev - pf) * mu_ref[...]

    r = ps[:, 0:da]
    k = ps[:, da:2 * da]
    v = ps[:, 2 * da:3 * da]
    o = 3 * da
    xw = ps[:, o:o + lora_w]
    xa = ps[:, o + lora_w:o + lora_w + lora_a]
    xg = ps[:, o + lora_w + lora_a:]

    w_log = -_softplus(-(w0_ref[...] + _dot(jnp.tanh(xw), ww2_ref[...]))) - 0.5
    ld = -jnp.exp(w_log)
    a = _sigmoid(a0_ref[...] + _dot(xa, wa2_ref[...]))
    g = _dot(_sigmoid(xg), wg2_ref[...])

    seg = seg_ref[...]
    segt = segt_ref[...]
    seg_sum = lambda t: _head_sum(t, seg, segt)
    kraw = k * kk_ref[...]
    kk = kraw * lax.rsqrt(seg_sum(kraw * kraw) + EPS)
    k2 = k * (1.0 + (a - 1.0) * ka_ref[...])
    b = kk * a

    tril_bf = _tri(c, False).astype(BF16)
    gam = _dot_exact_lhs(tril_bf, ld)
    gl = gam[c - 1:c, :]
    eng = jnp.exp(-gam)
    el = jnp.exp(gl - gam)
    fe_ref[0] = kk * jnp.exp(gam - ld)
    fe_ref[1] = r * jnp.exp(gam)
    fe_ref[2] = k2 * eng
    fe_ref[3] = b * eng
    fe_ref[4] = k2 * el
    fe_ref[5] = b * el
    fe_ref[6] = v
    egl = jnp.exp(gl)

    strict = _tri(c, True)
    incl = _tri(c, False)
    for h in range(heads):
        hs = slice(h * hd, (h + 1) * hd)
        lhs = jnp.concatenate([fe_ref[0, :, hs], fe_ref[1, :, hs]], axis=0)
        vh = fe_ref[6, :, hs]
        s0 = state_ref[h]
        sk = _dot_nt(lhs, fe_ref[2, :, hs])
        sb = _dot_nt(lhs, fe_ref[3, :, hs])
        pa = _dot_nt(lhs, s0)
        rhs = pa[:c] + _dot(jnp.where(strict, sk[:c], 0.0), vh)
        u = _unit_lower_solve(jnp.where(strict, sb[:c], 0.0), rhs)
        yh = pa[c:] + _dot(jnp.where(incl, sk[c:], 0.0), vh) - _dot(jnp.where(incl, sb[c:], 0.0), u)
        yh_ref[:, hs] = yh
        state_ref[h] = s0 * egl[:, hs] + _dot_tn(vh, fe_ref[4, :, hs]) - _dot_tn(u, fe_ref[5, :, hs])

    y = yh_ref[...]
    inv_n = 1.0 / hd
    mean = seg_sum(y) * inv_n
    dlt = y - mean
    var = seg_sum(dlt * dlt) * inv_n
    y = dlt * lax.rsqrt(var + RWKV_GN_EPS) * lng_ref[...] + lnb_ref[...]
    bonus = seg_sum(r * k2 * rk_ref[...]) * v
    y_ref[0] = ((y + bonus) * g).astype(y_ref.dtype)

    @pl.when(n == pl.num_programs(1) - 1)
    def _():
        shift_ref[0] = pf[c - 1:c, :]
        s_ref[0] = state_ref[...]


def _head_sum(t, seg, segt):
    return _dot_exact_rhs(_dot_exact_rhs(t, seg), segt)


def rwkv_mixer(p, shift0, s0, prm, heads, hd, lora_w, lora_a):
    b, t, cols = p.shape
    c = min(CHUNK, t)
    da = heads * hd
    seg = (jnp.arange(da)[:, None] // hd == jnp.arange(LANE)[None, :]).astype(BF16)
    segt = seg.T
    rowp = lambda a: a.reshape(1, -1)
    consts = [rowp(prm['mu_a']), rowp(prm['w0_a']), prm['w_w2'], rowp(prm['a0_a']), prm['w_a2'], prm['w_g2'],
              rowp(prm['k_k']), rowp(prm['k_a']), rowp(prm['r_k']), rowp(prm['lnx_g']), rowp(prm['lnx_b']),
              seg, segt]
    cspec = lambda a: pl.BlockSpec(a.shape, lambda i, j: (0, 0))
    y, shift, s = pl.pallas_call(
        functools.partial(_rwkv_kernel, heads=heads, hd=hd, lora_w=lora_w, lora_a=lora_a),
        grid=(b, t // c),
        in_specs=[pl.BlockSpec((1, c, cols), lambda i, j: (i, j, 0)),
                  pl.BlockSpec((1, 1, cols), lambda i, j: (i, 0, 0)),
                  pl.BlockSpec((1, heads, hd, hd), lambda i, j: (i, 0, 0, 0))] + [cspec(a) for a in consts],
        out_specs=[pl.BlockSpec((1, c, da), lambda i, j: (i, j, 0)),
                   pl.BlockSpec((1, 1, cols), lambda i, j: (i, 0, 0)),
                   pl.BlockSpec((1, heads, hd, hd), lambda i, j: (i, 0, 0, 0))],
        out_shape=[jax.ShapeDtypeStruct((b, t, da), BF16),
                   jax.ShapeDtypeStruct((b, 1, cols), F32),
                   jax.ShapeDtypeStruct((b, heads, hd, hd), F32)],
        scratch_shapes=[pltpu.VMEM((1, cols), F32),
                        pltpu.VMEM((heads, hd, hd), F32),
                        pltpu.VMEM((7, c, da), F32),
                        pltpu.VMEM((c, da), F32)],
        compiler_params=_cparams("parallel", "arbitrary"),
    )(p, shift0.reshape(b, 1, cols), s0, *consts)
    return y, shift.reshape(b, cols), s


def _gla_kernel(p_ref, s0_ref, wgk_ref, bgk_ref, ng_ref, wseg_ref,
                y_ref, s_ref,
                state_ref, xs_ref, *, heads, dk, dv, lora):
    n = pl.program_id(1)
    c = p_ref.shape[1]
    hk = heads * dk
    hv = heads * dv
    sub = min(SUB, c)

    @pl.when(n == 0)
    def _():
        state_ref[...] = s0_ref[0]

    pf = p_ref[0]
    q = pf[:, 0:hk] * (dk ** -0.5)
    k = pf[:, hk:2 * hk]
    v = pf[:, 2 * hk:2 * hk + hv]
    g = pf[:, 2 * hk + hv:2 * hk + 2 * hv]
    gk = pf[:, 2 * hk + 2 * hv:2 * hk + 2 * hv + lora]

    z = _dot(gk, wgk_ref[...]) + bgk_ref[...]
    glog = (jnp.minimum(z, 0.0) - jnp.log(1.0 + jnp.exp(-jnp.abs(z)))) * (1.0 / GK_NORMALIZER)
    gc = _dot_exact_lhs(_tri(c, False).astype(BF16), glog)
    gl = gc[c - 1:c, :]
    qg = q * jnp.exp(gc)
    kd = k * jnp.exp(gl - gc)
    egl = jnp.exp(gl)

    srow = lax.broadcasted_iota(jnp.int32, (sub, hk), 0)
    blocks = []
    for bi in range(c // sub):
        r0 = bi * sub
        q_i = q[r0:r0 + sub]
        gc_i = gc[r0:r0 + sub]
        k_i = k[r0:r0 + sub]
        v_i = v[r0:r0 + sub]
        for jj in range(sub):
            e = jnp.exp(jnp.minimum(gc_i - gc_i[jj:jj + 1], 0.0))
            x = jnp.where(srow >= jj, q_i * e * k_i[jj:jj + 1], 0.0)
            xs_ref[jj * sub:(jj + 1) * sub, :] = x.astype(BF16)
        red = jnp.dot(xs_ref[...], wseg_ref[...], preferred_element_type=F32)
        o_i = red[0:sub] * v_i[0:1]
        for jj in range(1, sub):
            o_i = o_i + red[jj * sub:(jj + 1) * sub] * v_i[jj:jj + 1]
        cols = []
        if bi > 0:
            ref_row = gc_i[0:1]
            qt = q_i * jnp.exp(gc_i - ref_row)
            kt = k[:r0] * jnp.exp(jnp.minimum(ref_row - gc[:r0], 0.0))
        for h in range(heads):
            ks = slice(h * dk, (h + 1) * dk)
            vs = slice(h * dv, (h + 1) * dv)
            o_h = _dot_nt(qg[r0:r0 + sub, ks], state_ref[h])
            if bi > 0:
                o_h = o_h + _dot(_dot_nt(qt[:, ks], kt[:, ks]), v[:r0, vs])
            cols.append(o_h)
        blocks.append(o_i + jnp.concatenate(cols, axis=-1))
    o = jnp.concatenate(blocks, axis=0) if len(blocks) > 1 else blocks[0]

    for h in range(heads):
        ks = slice(h * dk, (h + 1) * dk)
        vs = slice(h * dv, (h + 1) * dv)
        state_ref[h] = state_ref[h] * egl[:, ks] + _dot_tn(v[:, vs], kd[:, ks])
        o_h = o[:, vs]
        o_h = _rms(o_h, ng_ref[...])
        g_h = g[:, vs]
        y_ref[0, :, vs] = (o_h * (g_h * _sigmoid(g_h))).astype(y_ref.dtype)

    @pl.when(n == pl.num_programs(1) - 1)
    def _():
        s_ref[0] = state_ref[...]


def gla_mixer(p, s0, prm, heads, dk, dv, lora):
    b, t, cols = p.shape
    c = min(CHUNK, t)
    hk, hv = heads * dk, heads * dv
    wseg = (jnp.arange(hk)[:, None] // dk == jnp.arange(hv)[None, :] // dv).astype(BF16)
    consts = [prm['w_gk2'], prm['b_gk'].reshape(1, hk), prm['gla_norm_g'].reshape(1, dv), wseg]
    cspec = lambda a: pl.BlockSpec(a.shape, lambda i, j: (0, 0))
    sub = min(SUB, c)
    y, s = pl.pallas_call(
        functools.partial(_gla_kernel, heads=heads, dk=dk, dv=dv, lora=lora),
        grid=(b, t // c),
        in_specs=[pl.BlockSpec((1, c, cols), lambda i, j: (i, j, 0)),
                  pl.BlockSpec((1, heads, dv, dk), lambda i, j: (i, 0, 0, 0))] + [cspec(a) for a in consts],
        out_specs=[pl.BlockSpec((1, c, hv), lambda i, j: (i, j, 0)),
                   pl.BlockSpec((1, heads, dv, dk), lambda i, j: (i, 0, 0, 0))],
        out_shape=[jax.ShapeDtypeStruct((b, t, hv), BF16),
                   jax.ShapeDtypeStruct((b, heads, dv, dk), F32)],
        scratch_shapes=[pltpu.VMEM((heads, dv, dk), F32),
                        pltpu.VMEM((sub * sub, hk), BF16)],
        compiler_params=_cparams("parallel", "arbitrary"),
    )(p, jnp.swapaxes(s0, -1, -2), *consts)
    return y, jnp.swapaxes(s, -1, -2)


def _gdn_kernel(p_ref, conv0_ref, s0_ref, cw_ref, alog_ref, dtb_ref, ng_ref,
                y_ref, conv_ref, s_ref,
                xp_ref, state_ref, *, heads, dk, dv, conv_w):
    n = pl.program_id(1)
    c = p_ref.shape[1]
    cq = 2 * heads * dk + heads * dv
    pad = 8

    @pl.when(n == 0)
    def _():
        xp_ref[0:pad, :] = conv0_ref[0]
        state_ref[...] = s0_ref[0]

    xp_ref[pad:pad + c, :] = p_ref[0, :, 0:cq]
    base = pad - (conv_w - 1)
    conv = xp_ref[base:base + c, :] * cw_ref[0:1, :]
    for w in range(1, conv_w):
        conv = conv + xp_ref[base + w:base + w + c, :] * cw_ref[w:w + 1, :]
    tail = xp_ref[c:c + pad, :]
    xp_ref[0:pad, :] = tail
    qkv = conv * _sigmoid(conv)

    zab = p_ref[0, :, cq:]
    z = zab[:, 0:heads * dv]
    ab = zab[:, heads * dv:]
    beta = _sigmoid(ab)
    g = -jnp.exp(alog_ref[...]) * _softplus(ab + dtb_ref[...])
    tril = _tri(c, False)
    strict = _tri(c, True)
    gc = _dot_exact_lhs(tril.astype(BF16), g)
    triu_bf = (lax.broadcasted_iota(jnp.int32, (c, c), 0) <= lax.broadcasted_iota(jnp.int32, (c, c), 1)).astype(BF16)
    gct = None
    for part in _split(g, 3):
        d = lax.dot_general(part, triu_bf, (((0,), (0,)), ((), ())), preferred_element_type=F32)
        gct = d if gct is None else gct + d
    gl = gc[c - 1:c, :]
    eg = jnp.exp(gc)
    egl = jnp.exp(gl - gc)
    egl_last = jnp.exp(gl)

    for h in range(heads):
        q = qkv[:, h * dk:(h + 1) * dk]
        k = qkv[:, heads * dk + h * dk:heads * dk + (h + 1) * dk]
        v = qkv[:, 2 * heads * dk + h * dv:2 * heads * dk + (h + 1) * dv]
        q = q * lax.rsqrt(jnp.sum(q * q, axis=-1, keepdims=True) + EPS) * (dk ** -0.5)
        k = k * lax.rsqrt(jnp.sum(k * k, axis=-1, keepdims=True) + EPS)
        bcol = beta[:, heads + h:heads + h + 1]
        diff = gc[:, h:h + 1] - gct[h:h + 1, :]
        dec = jnp.where(tril, jnp.exp(jnp.where(tril, diff, 0.0)), 0.0)
        kb = k * bcol
        vb = v * bcol
        eg_h = eg[:, h:h + 1]
        lmat = jnp.where(strict, _dot_nt(kb, k) * dec, 0.0)
        uw = _unit_lower_solve(lmat, jnp.concatenate([vb, kb * eg_h], axis=-1))
        att = _dot_nt(q, k) * dec
        s0 = state_ref[h]
        v_new = uw[:, :dv] - _dot(uw[:, dv:], s0)
        o = _dot(q * eg_h, s0) + _dot(att, v_new)
        state_ref[h] = s0 * egl_last[:, h:h + 1] + _dot_tn(k * egl[:, h:h + 1], v_new)
        o = _rms(o, ng_ref[...])
        z_h = z[:, h * dv:(h + 1) * dv]
        y_ref[0, :, h * dv:(h + 1) * dv] = (o * (z_h * _sigmoid(z_h))).astype(y_ref.dtype)

    @pl.when(n == pl.num_programs(1) - 1)
    def _():
        conv_ref[0] = tail
        s_ref[0] = state_ref[...]


def gdn_mixer(p, conv0, s0, prm, heads, dk, dv, conv_w):
    b, t, cols = p.shape
    c = min(CHUNK, t)
    cq = 2 * heads * dk + heads * dv
    pad = 8
    lanep = lambda a: jnp.pad(a.reshape(1, -1), ((0, 0), (0, LANE - a.size)))
    conv0p = jnp.pad(conv0, ((0, 0), (pad - (conv_w - 1), 0), (0, 0)))
    cw = jnp.pad(prm['conv_w'], ((0, pad - conv_w), (0, 0)))
    consts = [cw, lanep(prm['a_log']), lanep(prm['dt_bias']), prm['gdn_norm_g'].reshape(1, dv)]
    cspec = lambda a: pl.BlockSpec(a.shape, lambda i, j: (0, 0))
    y, conv, s = pl.pallas_call(
        functools.partial(_gdn_kernel, heads=heads, dk=dk, dv=dv, conv_w=conv_w),
        grid=(b, t // c),
        in_specs=[pl.BlockSpec((1, c, cols), lambda i, j: (i, j, 0)),
                  pl.BlockSpec((1, pad, cq), lambda i, j: (i, 0, 0)),
                  pl.BlockSpec((1, heads, dk, dv), lambda i, j: (i, 0, 0, 0))] + [cspec(a) for a in consts],
        out_specs=[pl.BlockSpec((1, c, heads * dv), lambda i, j: (i, j, 0)),
                   pl.BlockSpec((1, pad, cq), lambda i, j: (i, 0, 0)),
                   pl.BlockSpec((1, heads, dk, dv), lambda i, j: (i, 0, 0, 0))],
        out_shape=[jax.ShapeDtypeStruct((b, t, heads * dv), BF16),
                   jax.ShapeDtypeStruct((b, pad, cq), F32),
                   jax.ShapeDtypeStruct((b, heads, dk, dv), F32)],
        scratch_shapes=[pltpu.VMEM((c + pad, cq), F32),
                        pltpu.VMEM((heads, dk, dv), F32)],
        compiler_params=_cparams("parallel", "arbitrary"),
    )(p, conv0p, s0, *consts)
    return y, conv[:, pad - (conv_w - 1):, :], s


D_MODEL = 2048
DEPTH = 2
H_A, N_A = 12, 64
D_A = H_A * N_A
LORA_W, LORA_A, LORA_G = 64, 64, 128
H_B, DK_B, DV_B, LORA_GK = 4, 64, 128, 16
H_C, DK_C, DV_C, CONV_W = 6, 128, 128, 4
A_COLS = 3 * D_A + LORA_W + LORA_A + LORA_G
B_COLS = 2 * H_B * DK_B + 2 * H_B * DV_B + LORA_GK
C_QKV = 2 * H_C * DK_C + H_C * DV_C
C_COLS = C_QKV + H_C * DV_C + 2 * H_C
H_X = 4


def _pad_cols(w):
    n = w.shape[-1]
    return jnp.pad(w, ((0, 0), (0, -n % LANE)))


def _row_tile(m, cap):
    t = min(m, cap)
    while m % t:
        t //= 2
    return t


def _col_tile(n, cap):
    if n <= 2 * cap:
        return n
    best = LANE
    for t in range(LANE, cap + 1, LANE):
        if n % t == 0:
            best = t
    return best


def _trunk(x, mem_k, mem_v, st_shift, st_rwkv, st_gla, st_conv, st_gdn, lw, final_g):
    b, t, d = x.shape
    m = b * t
    xf = x.reshape(m, d)
    tm = _row_tile(m, 1024)
    tm_s = _row_tile(m, 512)
    tq = _row_tile(t, 512)
    outs = [[] for _ in range(5)]
    for l in range(DEPTH):
        w = lw[l]
        proj = lambda wi: norm_matmul(xf, w['norm1_g'], wi, tm, _col_tile(wi.shape[1], 896)).reshape(b, t, -1)
        ya, sh, sa = rwkv_mixer(proj(w['w_in_a']), st_shift[l], st_rwkv[l], w, H_A, N_A, LORA_W, LORA_A)
        yb, sb = gla_mixer(proj(w['w_in_b']), st_gla[l], w, H_B, DK_B, DV_B, LORA_GK)
        yc, cb, sc = gdn_mixer(proj(w['w_in_c']), st_conv[l], st_gdn[l], w, H_C, DK_C, DV_C, CONV_W)
        x1 = out_proj(xf, ya.reshape(m, -1), yb.reshape(m, -1), yc.reshape(m, -1),
                      w['w_out_a'], w['w_out_b'], w['w_out_c'], tm_s)
        x2 = cross_attn(x1.reshape(b, t, d), w['normx_g'], w['wq_x'], w['wo_x'], mem_k[l], mem_v[l], H_X, tq)
        xf = mlp(x2.reshape(m, d), w['norm3_g'], final_g, w['w_up'], w['w_down'], tm_s, 512,
                 final_norm=(l == DEPTH - 1))
        for lst, val in zip(outs, (sh, sa, sb, cb, sc)):
            lst.append(val)
    return (xf.reshape(b, t, d),) + tuple(jnp.stack(o) for o in outs)


def kernel(x_prompt, x_sample, state_rwkv_shift, state_rwkv, state_gla, state_gdn_conv, state_gdn, cache_mem_k, cache_mem_v, mem_prompt, norm1_g, w_in, mu_a, w0_a, w_w2, a0_a, w_a2, w_g2, k_k, k_a, r_k, lnx_g, lnx_b, w_gk2, b_gk, gla_norm_g, conv_w, a_log, dt_bias, gdn_norm_g, w_out, normx_g, normm_g, wq_x, wk_x, wv_x, wo_x, norm3_g, w_up, w_down, final_g):
    bf = lambda a: a.astype(BF16)
    lw = []
    for l in range(DEPTH):
        wi = w_in[l]
        lw.append(dict(
            norm1_g=norm1_g[l],
            w_in_a=bf(wi[:, :A_COLS]),
            w_in_b=bf(_pad_cols(wi[:, A_COLS:A_COLS + B_COLS])),
            w_in_c=bf(_pad_cols(wi[:, A_COLS + B_COLS:])),
            mu_a=mu_a[l], w0_a=w0_a[l], w_w2=w_w2[l], a0_a=a0_a[l], w_a2=w_a2[l], w_g2=w_g2[l],
            k_k=k_k[l], k_a=k_a[l], r_k=r_k[l], lnx_g=lnx_g[l], lnx_b=lnx_b[l],
            w_gk2=w_gk2[l], b_gk=b_gk[l], gla_norm_g=gla_norm_g[l],
            conv_w=conv_w[l], a_log=a_log[l], dt_bias=dt_bias[l], gdn_norm_g=gdn_norm_g[l],
            w_out_a=bf(w_out[l, :D_A]), w_out_b=bf(w_out[l, D_A:D_A + H_B * DV_B]),
            w_out_c=bf(w_out[l, D_A + H_B * DV_B:]),
            normx_g=normx_g[l], wq_x=bf(wq_x[l]), wo_x=bf(wo_x[l]),
            norm3_g=norm3_g[l], w_up=bf(w_up[l]), w_down=bf(w_down[l])))

    b, n_mem, d = mem_prompt.shape
    memf = mem_prompt.reshape(b * n_mem, d)
    hd_x = d // H_X
    tmm = _row_tile(b * n_mem, 1024)
    p_mem_k = jnp.stack([norm_matmul(memf, normm_g[l], bf(wk_x[l]), tmm, 512).reshape(b, n_mem, H_X, hd_x)
                         for l in range(DEPTH)])
    p_mem_v = jnp.stack([norm_matmul(memf, normm_g[l], bf(wv_x[l]), tmm, 512).reshape(b, n_mem, H_X, hd_x)
                         for l in range(DEPTH)])

    zeros = lambda *s: jnp.zeros((DEPTH, b) + s, F32)
    flat_mem = lambda a: bf(a.reshape(a.shape[0], a.shape[1], n_mem, d))
    y_prompt, p_shift, p_rwkv, p_gla, p_conv, p_gdn = _trunk(
        x_prompt, flat_mem(p_mem_k), flat_mem(p_mem_v),
        zeros(A_COLS), zeros(H_A, N_A, N_A), zeros(H_B, DK_B, DV_B), zeros(CONV_W - 1, C_QKV),
        zeros(H_C, DK_C, DV_C), lw, final_g)
    y_sample, s_shift, s_rwkv, s_gla, s_conv, s_gdn = _trunk(
        x_sample, flat_mem(cache_mem_k), flat_mem(cache_mem_v),
        state_rwkv_shift, state_rwkv, state_gla, state_gdn_conv, state_gdn, lw, final_g)
    return (y_prompt, y_sample, p_shift, p_rwkv, p_gla, p_conv, p_gdn, p_mem_k, p_mem_v,
            s_shift, s_rwkv, s_gla, s_conv, s_gdn)
```

```python
import functools

import jax
import jax.numpy as jnp
from jax import lax
from jax.experimental import pallas as pl
from jax.experimental.pallas import tpu as pltpu

F32 = jnp.float32
BF16 = jnp.bfloat16

EPS = 1e-6
RWKV_GN_EPS = 64e-5
GK_NORMALIZER = 16.0
CHUNK = 64
SUB = 16
LANE = 128
VMEM_LIMIT = 56 * 1024 * 1024


def _cparams(*sem):
    return pltpu.CompilerParams(dimension_semantics=sem, vmem_limit_bytes=VMEM_LIMIT)


def _dot(a, b):
    return jnp.dot(a.astype(BF16), b.astype(BF16), preferred_element_type=F32)


def _dot_nt(a, b):
    return lax.dot_general(a.astype(BF16), b.astype(BF16), (((1,), (1,)), ((), ())),
                           preferred_element_type=F32)


def _dot_tn(a, b):
    return lax.dot_general(a.astype(BF16), b.astype(BF16), (((0,), (0,)), ((), ())),
                           preferred_element_type=F32)


def _split(x, parts):
    out = []
    r = x
    for _ in range(parts):
        p = r.astype(BF16)
        out.append(p)
        r = r - p.astype(F32)
    return out


def _dot_exact_lhs(l_bf, x, parts=3):
    acc = None
    for p in _split(x, parts):
        d = jnp.dot(l_bf, p, preferred_element_type=F32)
        acc = d if acc is None else acc + d
    return acc


def _dot_exact_rhs(x, r_bf, parts=2):
    acc = None
    for p in _split(x, parts):
        d = jnp.dot(p, r_bf, preferred_element_type=F32)
        acc = d if acc is None else acc + d
    return acc


def _rms(x, g):
    ms = jnp.mean(x * x, axis=-1, keepdims=True)
    return x * lax.rsqrt(ms + EPS) * g


def _sigmoid(x):
    return 1.0 / (1.0 + jnp.exp(-x))


def _softplus(x):
    return jnp.maximum(x, 0.0) + jnp.log(1.0 + jnp.exp(-jnp.abs(x)))


def _tri(c, strict):
    r = lax.broadcasted_iota(jnp.int32, (c, c), 0)
    q = lax.broadcasted_iota(jnp.int32, (c, c), 1)
    return (r > q) if strict else (r >= q)


INV_BASE = 8


def _unit_lower_solve(ls, rhss):
    c = ls[0].shape[0]
    r = lax.broadcasted_iota(jnp.int32, (c, c), 0)
    q = lax.broadcasted_iota(jnp.int32, (c, c), 1)
    eye = (r == q).astype(F32)
    base = min(INV_BASE, c)
    diag_blk = (r // base) == (q // base)
    ns = [jnp.where(diag_blk, -l, 0.0).astype(BF16) for l in ls]
    ts = [eye + n.astype(F32) for n in ns]
    p = 2
    while p < base:
        ns = [_dot(n, n).astype(BF16) for n in ns]
        ts = [t + _dot(t, n) for t, n in zip(ts, ns)]
        p *= 2
    s = base
    while s < c:
        sub_blk = ((r // s) % 2 == 1) & ((q // s) == (r // s) - 1)
        cs = [jnp.where(sub_blk, l, 0.0).astype(BF16) for l in ls]
        tb = [t.astype(BF16) for t in ts]
        cts = [_dot(cm, t).astype(BF16) for cm, t in zip(cs, tb)]
        ts = [t - _dot(b, ct) for t, b, ct in zip(ts, tb, cts)]
        s *= 2
    return [_dot(t, rh) for t, rh in zip(ts, rhss)]


def _norm_matmul_kernel(x_ref, g_ref, w_ref, o_ref, h_ref):
    @pl.when(pl.program_id(1) == 0)
    def _():
        h_ref[...] = _rms(x_ref[...], g_ref[...]).astype(BF16)

    o_ref[...] = jnp.dot(h_ref[...], w_ref[...], preferred_element_type=F32).astype(o_ref.dtype)


def norm_matmul(x, g, w, tm, tn):
    m, k = x.shape
    n = w.shape[1]
    return pl.pallas_call(
        _norm_matmul_kernel,
        grid=(m // tm, n // tn),
        in_specs=[pl.BlockSpec((tm, k), lambda i, j: (i, 0)),
                  pl.BlockSpec((1, k), lambda i, j: (0, 0)),
                  pl.BlockSpec((k, tn), lambda i, j: (0, j))],
        out_specs=pl.BlockSpec((tm, tn), lambda i, j: (i, j)),
        out_shape=jax.ShapeDtypeStruct((m, n), F32),
        scratch_shapes=[pltpu.VMEM((tm, k), BF16)],
        compiler_params=_cparams("parallel", "arbitrary"),
    )(x, g.reshape(1, k), w)


def _split_cols_kernel(w_ref, a_ref, b_ref, c_ref, *, na, nb):
    n = w_ref.shape[2]
    nc = n - na - nb
    a_ref[0] = w_ref[0, :, 0:na].astype(BF16)
    b_ref[0, :, 0:nb] = w_ref[0, :, na:na + nb].astype(BF16)
    b_ref[0, :, nb:] = jnp.zeros((b_ref.shape[1], b_ref.shape[2] - nb), BF16)
    c_ref[0, :, 0:nc] = w_ref[0, :, na + nb:].astype(BF16)
    c_ref[0, :, nc:] = jnp.zeros((c_ref.shape[1], c_ref.shape[2] - nc), BF16)


def split_cols(w, na, nb, tk):
    depth, k, n = w.shape
    nc = n - na - nb
    widths = [na, nb + (-nb % LANE), nc + (-nc % LANE)]
    return pl.pallas_call(
        functools.partial(_split_cols_kernel, na=na, nb=nb),
        grid=(depth, k // tk),
        in_specs=[pl.BlockSpec((1, tk, n), lambda l, i: (l, i, 0))],
        out_specs=[pl.BlockSpec((1, tk, wd), lambda l, i: (l, i, 0)) for wd in widths],
        out_shape=[jax.ShapeDtypeStruct((depth, k, wd), BF16) for wd in widths],
        compiler_params=_cparams("parallel", "parallel"),
    )(w)


def _out_proj_kernel(x_ref, ya_ref, yb_ref, yc_ref, wa_ref, wb_ref, wc_ref, o_ref):
    acc = jnp.dot(ya_ref[...], wa_ref[...], preferred_element_type=F32)
    acc += jnp.dot(yb_ref[...], wb_ref[...], preferred_element_type=F32)
    acc += jnp.dot(yc_ref[...], wc_ref[...], preferred_element_type=F32)
    o_ref[...] = x_ref[...] + acc


def out_proj(x, ya, yb, yc, wa, wb, wc, tm):
    m, d = x.shape
    row = lambda a: pl.BlockSpec((tm, a.shape[1]), lambda i: (i, 0))
    full = lambda a: pl.BlockSpec(a.shape, lambda i: (0, 0))
    return pl.pallas_call(
        _out_proj_kernel,
        grid=(m // tm,),
        in_specs=[row(x), row(ya), row(yb), row(yc), full(wa), full(wb), full(wc)],
        out_specs=row(x),
        out_shape=jax.ShapeDtypeStruct((m, d), F32),
        compiler_params=_cparams("parallel"),
    )(x, ya, yb, yc, wa, wb, wc)


def _cross_attn_kernel(x_ref, g_ref, wq_ref, wo_ref, k_ref, v_ref, o_ref, *, heads):
    x = x_ref[0]
    d = x.shape[-1]
    hd = d // heads
    h = _rms(x, g_ref[...]).astype(BF16)
    q = jnp.dot(h, wq_ref[...], preferred_element_type=F32).astype(BF16)
    outs = []
    for i in range(heads):
        hs = slice(i * hd, (i + 1) * hd)
        s = _dot_nt(q[:, hs], k_ref[0, :, hs]) * (hd ** -0.5)
        s = s - jnp.max(s, axis=-1, keepdims=True)
        e = jnp.exp(s)
        pr = e / jnp.sum(e, axis=-1, keepdims=True)
        outs.append(jnp.dot(pr.astype(BF16), v_ref[0, :, hs], preferred_element_type=F32).astype(BF16))
    o = jnp.concatenate(outs, axis=-1)
    o_ref[0] = x + jnp.dot(o, wo_ref[...], preferred_element_type=F32)


def cross_attn(x, g, wq, wo, mk, mv, heads, tq):
    b, t, d = x.shape
    n_mem = mk.shape[1]
    full = lambda a: pl.BlockSpec(a.shape, lambda i, j: (0, 0))
    return pl.pallas_call(
        functools.partial(_cross_attn_kernel, heads=heads),
        grid=(b, t // tq),
        in_specs=[pl.BlockSpec((1, tq, d), lambda i, j: (i, j, 0)),
                  pl.BlockSpec((1, d), lambda i, j: (0, 0)),
                  full(wq), full(wo),
                  pl.BlockSpec((1, n_mem, d), lambda i, j: (i, 0, 0)),
                  pl.BlockSpec((1, n_mem, d), lambda i, j: (i, 0, 0))],
        out_specs=pl.BlockSpec((1, tq, d), lambda i, j: (i, j, 0)),
        out_shape=jax.ShapeDtypeStruct((b, t, d), F32),
        compiler_params=_cparams("parallel", "parallel"),
    )(x, g.reshape(1, d), wq, wo, mk, mv)


def _mlp_kernel(x_ref, g_ref, fg_ref, wu_ref, wd_ref, o_ref, h_ref, acc_ref, *, final_norm):
    f = pl.program_id(1)

    @pl.when(f == 0)
    def _():
        h_ref[...] = _rms(x_ref[...], g_ref[...]).astype(BF16)
        acc_ref[...] = jnp.zeros_like(acc_ref)

    u = jnp.maximum(jnp.dot(h_ref[...], wu_ref[...], preferred_element_type=F32), 0.0)
    acc_ref[...] += jnp.dot((u * u).astype(BF16), wd_ref[...], preferred_element_type=F32)

    @pl.when(f == pl.num_programs(1) - 1)
    def _():
        y = x_ref[...] + acc_ref[...]
        if final_norm:
            y = _rms(y, fg_ref[...])
        o_ref[...] = y


def mlp(x, g, fg, wu, wd, tm, tf, final_norm):
    m, d = x.shape
    dff = wu.shape[1]
    return pl.pallas_call(
        functools.partial(_mlp_kernel, final_norm=final_norm),
        grid=(m // tm, dff // tf),
        in_specs=[pl.BlockSpec((tm, d), lambda i, j: (i, 0)),
                  pl.BlockSpec((1, d), lambda i, j: (0, 0)),
                  pl.BlockSpec((1, d), lambda i, j: (0, 0)),
                  pl.BlockSpec((d, tf), lambda i, j: (0, j)),
                  pl.BlockSpec((tf, d), lambda i, j: (j, 0))],
        out_specs=pl.BlockSpec((tm, d), lambda i, j: (i, 0)),
        out_shape=jax.ShapeDtypeStruct((m, d), F32),
        scratch_shapes=[pltpu.VMEM((tm, d), BF16), pltpu.VMEM((tm, d), F32)],
        compiler_params=_cparams("parallel", "arbitrary"),
    )(x, g.reshape(1, d), fg.reshape(1, d), wu, wd)


def _rwkv_kernel(p_ref, shift0_ref, s0_ref, mu_ref, w0_ref, ww2_ref, a0_ref, wa2_ref, wg2_ref,
                 kk_ref, ka_ref, rk_ref, lng_ref, lnb_ref, seg_ref, segt_ref,
                 y_ref, shift_ref, s_ref,
                 carry_ref, state_ref, fe_ref, yh_ref, *, heads, hd, lora_w, lora_a):
    n = pl.program_id(1)
    c = p_ref.shape[1]
    da = heads * hd

    @pl.when(n == 0)
    def _():
        carry_ref[...] = shift0_ref[0]
        state_ref[...] = s0_ref[0]

    pf = p_ref[0]
    row = lax.broadcasted_iota(jnp.int32, pf.shape, 0)
    prev = jnp.where(row == 0, carry_ref[...], pltpu.roll(pf, shift=1, axis=0))
    carry_ref[...] = pf[c - 1:c, :]
    ps = pf + (prev - pf) * mu_ref[...]

    r = ps[:, 0:da]
    k = ps[:, da:2 * da]
    v = ps[:, 2 * da:3 * da]
    o = 3 * da
    xw = ps[:, o:o + lora_w]
    xa = ps[:, o + lora_w:o + lora_w + lora_a]
    xg = ps[:, o + lora_w + lora_a:]

    w_log = -_softplus(-(w0_ref[...] + _dot(jnp.tanh(xw), ww2_ref[...]))) - 0.5
    ld = -jnp.exp(w_log)
    a = _sigmoid(a0_ref[...] + _dot(xa, wa2_ref[...]))
    g = _dot(_sigmoid(xg), wg2_ref[...])

    seg = seg_ref[...]
    segt = segt_ref[...]
    seg_sum = lambda t: _head_sum(t, seg, segt)
    kraw = k * kk_ref[...]
    kk = kraw * lax.rsqrt(seg_sum(kraw * kraw) + EPS)
    k2 = k * (1.0 + (a - 1.0) * ka_ref[...])
    b = kk * a

    tril_bf = _tri(c, False).astype(BF16)
    gam = _dot_exact_lhs(tril_bf, ld)
    gl = gam[c - 1:c, :]
    eng = jnp.exp(-gam)
    el = jnp.exp(gl - gam)
    fe_ref[0] = kk * jnp.exp(gam - ld)
    fe_ref[1] = r * jnp.exp(gam)
    fe_ref[2] = k2 * eng
    fe_ref[3] = b * eng
    fe_ref[4] = k2 * el
    fe_ref[5] = b * el
    fe_ref[6] = v
    egl = jnp.exp(gl)

    strict = _tri(c, True)
    incl = _tri(c, False)
    hr = range(heads)
    hsl = [slice(h * hd, (h + 1) * hd) for h in hr]
    lhs = [jnp.concatenate([fe_ref[0, :, s], fe_ref[1, :, s]], axis=0).astype(BF16) for s in hsl]
    vh = [fe_ref[6, :, s].astype(BF16) for s in hsl]
    s0 = [state_ref[h] for h in hr]
    sk = [_dot_nt(lhs[h], fe_ref[2, :, hsl[h]]) for h in hr]
    sb = [_dot_nt(lhs[h], fe_ref[3, :, hsl[h]]) for h in hr]
    pa = [_dot_nt(lhs[h], s0[h]) for h in hr]
    rhs = [pa[h][:c] + _dot(jnp.where(strict, sk[h][:c], 0.0), vh[h]) for h in hr]
    u = _unit_lower_solve([jnp.where(strict, sb[h][:c], 0.0) for h in hr], rhs)
    ya = [_dot(jnp.where(incl, sk[h][c:], 0.0), vh[h]) for h in hr]
    yb = [_dot(jnp.where(incl, sb[h][c:], 0.0), u[h]) for h in hr]
    sa = [_dot_tn(vh[h], fe_ref[4, :, hsl[h]]) for h in hr]
    sc = [_dot_tn(u[h], fe_ref[5, :, hsl[h]]) for h in hr]
    for h in hr:
        yh_ref[:, hsl[h]] = pa[h][c:] + ya[h] - yb[h]
        state_ref[h] = s0[h] * egl[:, hsl[h]] + sa[h] - sc[h]

    y = yh_ref[...]
    inv_n = 1.0 / hd
    mean = seg_sum(y) * inv_n
    dlt = y - mean
    var = seg_sum(dlt * dlt) * inv_n
    y = dlt * lax.rsqrt(var + RWKV_GN_EPS) * lng_ref[...] + lnb_ref[...]
    bonus = seg_sum(r * k2 * rk_ref[...]) * v
    y_ref[0] = ((y + bonus) * g).astype(y_ref.dtype)

    @pl.when(n == pl.num_programs(1) - 1)
    def _():
        shift_ref[0] = pf[c - 1:c, :]
        s_ref[0] = state_ref[...]


def _head_sum(t, seg, segt):
    return _dot_exact_rhs(_dot_exact_rhs(t, seg), segt)


def rwkv_mixer(p, shift0, s0, prm, heads, hd, lora_w, lora_a):
    b, t, cols = p.shape
    c = min(CHUNK, t)
    da = heads * hd
    seg = (jnp.arange(da)[:, None] // hd == jnp.arange(LANE)[None, :]).astype(BF16)
    segt = seg.T
    rowp = lambda a: a.reshape(1, -1)
    consts = [rowp(prm['mu_a']), rowp(prm['w0_a']), prm['w_w2'], rowp(prm['a0_a']), prm['w_a2'], prm['w_g2'],
              rowp(prm['k_k']), rowp(prm['k_a']), rowp(prm['r_k']), rowp(prm['lnx_g']), rowp(prm['lnx_b']),
              seg, segt]
    cspec = lambda a: pl.BlockSpec(a.shape, lambda i, j: (0, 0))
    y, shift, s = pl.pallas_call(
        functools.partial(_rwkv_kernel, heads=heads, hd=hd, lora_w=lora_w, lora_a=lora_a),
        grid=(b, t // c),
        in_specs=[pl.BlockSpec((1, c, cols), lambda i, j: (i, j, 0)),
                  pl.BlockSpec((1, 1, cols), lambda i, j: (i, 0, 0)),
                  pl.BlockSpec((1, heads, hd, hd), lambda i, j: (i, 0, 0, 0))] + [cspec(a) for a in consts],
        out_specs=[pl.BlockSpec((1, c, da), lambda i, j: (i, j, 0)),
                   pl.BlockSpec((1, 1, cols), lambda i, j: (i, 0, 0)),
                   pl.BlockSpec((1, heads, hd, hd), lambda i, j: (i, 0, 0, 0))],
        out_shape=[jax.ShapeDtypeStruct((b, t, da), BF16),
                   jax.ShapeDtypeStruct((b, 1, cols), F32),
                   jax.ShapeDtypeStruct((b, heads, hd, hd), F32)],
        scratch_shapes=[pltpu.VMEM((1, cols), F32),
                        pltpu.VMEM((heads, hd, hd), F32),
                        pltpu.VMEM((7, c, da), F32),
                        pltpu.VMEM((c, da), F32)],
        compiler_params=_cparams("parallel", "arbitrary"),
    )(p, shift0.reshape(b, 1, cols), s0, *consts)
    return y, shift.reshape(b, cols), s


def _gla_kernel(p_ref, s0_ref, wgk_ref, bgk_ref, ng_ref, wseg_ref,
                y_ref, s_ref,
                state_ref, *, heads, dk, dv, lora):
    n = pl.program_id(1)
    c = p_ref.shape[1]
    hk = heads * dk
    hv = heads * dv
    sub = min(SUB, c)

    @pl.when(n == 0)
    def _():
        state_ref[...] = s0_ref[0]

    pf = p_ref[0]
    q = pf[:, 0:hk] * (dk ** -0.5)
    k = pf[:, hk:2 * hk]
    v = pf[:, 2 * hk:2 * hk + hv]
    g = pf[:, 2 * hk + hv:2 * hk + 2 * hv]
    gk = pf[:, 2 * hk + 2 * hv:2 * hk + 2 * hv + lora]

    z = _dot(gk, wgk_ref[...]) + bgk_ref[...]
    glog = (jnp.minimum(z, 0.0) - jnp.log(1.0 + jnp.exp(-jnp.abs(z)))) * (1.0 / GK_NORMALIZER)
    gc = _dot_exact_lhs(_tri(c, False).astype(BF16), glog)
    gl = gc[c - 1:c, :]
    qg = q * jnp.exp(gc)
    kd = k * jnp.exp(gl - gc)
    egl = jnp.exp(gl)

    hr = range(heads)
    ksl = [slice(h * dk, (h + 1) * dk) for h in hr]
    vsl = [slice(h * dv, (h + 1) * dv) for h in hr]
    nb = c // sub
    srow = lax.broadcasted_iota(jnp.int32, (sub, hk), 0)
    vb = v.astype(BF16)
    s0 = [state_ref[h] for h in hr]
    wseg = wseg_ref[...]

    reds = []
    for bi in range(nb):
        r0 = bi * sub
        q_i, gc_i, k_i = q[r0:r0 + sub], gc[r0:r0 + sub], k[r0:r0 + sub]
        xs = []
        for jj in range(sub):
            e = jnp.exp(jnp.minimum(gc_i - gc_i[jj:jj + 1], 0.0))
            xs.append(jnp.where(srow >= jj, q_i * e * k_i[jj:jj + 1], 0.0).astype(BF16))
        reds.append(jnp.dot(jnp.concatenate(xs, axis=0), wseg, preferred_element_type=F32))

    qts, kts = [None], [None]
    for bi in range(1, nb):
        r0 = bi * sub
        ref_row = gc[r0:r0 + 1]
        qts.append((q[r0:r0 + sub] * jnp.exp(gc[r0:r0 + sub] - ref_row)).astype(BF16))
        kts.append((k[:r0] * jnp.exp(jnp.minimum(ref_row - gc[:r0], 0.0))).astype(BF16))
    o_in = [_dot_nt(qg[:, ksl[h]], s0[h]) for h in hr]
    sc = [[_dot_nt(qts[bi][:, ksl[h]], kts[bi][:, ksl[h]]).astype(BF16) for h in hr] for bi in range(1, nb)]
    off = [[_dot(sc[bi - 1][h], vb[:bi * sub, vsl[h]]) for h in hr] for bi in range(1, nb)]
    s_up = [_dot_tn(vb[:, vsl[h]], kd[:, ksl[h]]) for h in hr]

    blocks = []
    for bi in range(nb):
        r0 = bi * sub
        v_i = v[r0:r0 + sub]
        o_i = reds[bi][0:sub] * v_i[0:1]
        for jj in range(1, sub):
            o_i = o_i + reds[bi][jj * sub:(jj + 1) * sub] * v_i[jj:jj + 1]
        if bi > 0:
            o_i = o_i + jnp.concatenate(off[bi - 1], axis=-1)
        blocks.append(o_i)
    o = (jnp.concatenate(blocks, axis=0) if nb > 1 else blocks[0]) + jnp.concatenate(o_in, axis=-1)

    for h in hr:
        state_ref[h] = s0[h] * egl[:, ksl[h]] + s_up[h]
        o_h = _rms(o[:, vsl[h]], ng_ref[...])
        g_h = g[:, vsl[h]]
        y_ref[0, :, vsl[h]] = (o_h * (g_h * _sigmoid(g_h))).astype(y_ref.dtype)

    @pl.when(n == pl.num_programs(1) - 1)
    def _():
        s_ref[0] = state_ref[...]


def gla_mixer(p, s0, prm, heads, dk, dv, lora):
    b, t, cols = p.shape
    c = min(CHUNK, t)
    hk, hv = heads * dk, heads * dv
    wseg = (jnp.arange(hk)[:, None] // dk == jnp.arange(hv)[None, :] // dv).astype(BF16)
    consts = [prm['w_gk2'], prm['b_gk'].reshape(1, hk), prm['gla_norm_g'].reshape(1, dv), wseg]
    cspec = lambda a: pl.BlockSpec(a.shape, lambda i, j: (0, 0))
    y, s = pl.pallas_call(
        functools.partial(_gla_kernel, heads=heads, dk=dk, dv=dv, lora=lora),
        grid=(b, t // c),
        in_specs=[pl.BlockSpec((1, c, cols), lambda i, j: (i, j, 0)),
                  pl.BlockSpec((1, heads, dv, dk), lambda i, j: (i, 0, 0, 0))] + [cspec(a) for a in consts],
        out_specs=[pl.BlockSpec((1, c, hv), lambda i, j: (i, j, 0)),
                   pl.BlockSpec((1, heads, dv, dk), lambda i, j: (i, 0, 0, 0))],
        out_shape=[jax.ShapeDtypeStruct((b, t, hv), BF16),
                   jax.ShapeDtypeStruct((b, heads, dv, dk), F32)],
        scratch_shapes=[pltpu.VMEM((heads, dv, dk), F32)],
        compiler_params=_cparams("parallel", "arbitrary"),
    )(p, jnp.swapaxes(s0, -1, -2), *consts)
    return y, jnp.swapaxes(s, -1, -2)


def _gdn_kernel(p_ref, conv0_ref, s0_ref, cw_ref, alog_ref, dtb_ref, ng_ref,
                y_ref, conv_ref, s_ref,
                xp_ref, state_ref, *, heads, dk, dv, conv_w):
    n = pl.program_id(1)
    c = p_ref.shape[1]
    cq = 2 * heads * dk + heads * dv
    pad = 8

    @pl.when(n == 0)
    def _():
        xp_ref[0:pad, :] = conv0_ref[0]
        state_ref[...] = s0_ref[0]

    xp_ref[pad:pad + c, :] = p_ref[0, :, 0:cq]
    base = pad - (conv_w - 1)
    conv = xp_ref[base:base + c, :] * cw_ref[0:1, :]
    for w in range(1, conv_w):
        conv = conv + xp_ref[base + w:base + w + c, :] * cw_ref[w:w + 1, :]
    tail = xp_ref[c:c + pad, :]
    xp_ref[0:pad, :] = tail
    qkv = conv * _sigmoid(conv)

    zab = p_ref[0, :, cq:]
    z = zab[:, 0:heads * dv]
    ab = zab[:, heads * dv:]
    beta = _sigmoid(ab)
    g = -jnp.exp(alog_ref[...]) * _softplus(ab + dtb_ref[...])
    tril = _tri(c, False)
    strict = _tri(c, True)
    gc = _dot_exact_lhs(tril.astype(BF16), g)
    triu_bf = (lax.broadcasted_iota(jnp.int32, (c, c), 0) <= lax.broadcasted_iota(jnp.int32, (c, c), 1)).astype(BF16)
    gct = None
    for part in _split(g, 3):
        d = lax.dot_general(part, triu_bf, (((0,), (0,)), ((), ())), preferred_element_type=F32)
        gct = d if gct is None else gct + d
    gl = gc[c - 1:c, :]
    eg = jnp.exp(gc)
    egl = jnp.exp(gl - gc)
    egl_last = jnp.exp(gl)

    hr = range(heads)
    qs, ks, kbs, rhss, decs = [], [], [], [], []
    for h in hr:
        q = qkv[:, h * dk:(h + 1) * dk]
        k = qkv[:, heads * dk + h * dk:heads * dk + (h + 1) * dk]
        v = qkv[:, 2 * heads * dk + h * dv:2 * heads * dk + (h + 1) * dv]
        q = q * lax.rsqrt(jnp.sum(q * q, axis=-1, keepdims=True) + EPS) * (dk ** -0.5)
        k = k * lax.rsqrt(jnp.sum(k * k, axis=-1, keepdims=True) + EPS)
        bcol = beta[:, heads + h:heads + h + 1]
        diff = gc[:, h:h + 1] - gct[h:h + 1, :]
        decs.append(jnp.where(tril, jnp.exp(jnp.where(tril, diff, 0.0)), 0.0))
        kb = k * bcol
        qs.append(q)
        ks.append(k)
        kbs.append(kb.astype(BF16))
        rhss.append(jnp.concatenate([v * bcol, kb * eg[:, h:h + 1]], axis=-1))
    kbf = [k.astype(BF16) for k in ks]
    lmat = [jnp.where(strict, _dot_nt(kbs[h], kbf[h]) * decs[h], 0.0) for h in hr]
    att = [(_dot_nt(qs[h], kbf[h]) * decs[h]).astype(BF16) for h in hr]
    uw = _unit_lower_solve(lmat, rhss)
    s0 = [state_ref[h] for h in hr]
    s0b = [s.astype(BF16) for s in s0]
    v_new = [uw[h][:, :dv] - _dot(uw[h][:, dv:], s0b[h]) for h in hr]
    o_in = [_dot(qs[h] * eg[:, h:h + 1], s0b[h]) for h in hr]
    o_at = [_dot(att[h], v_new[h]) for h in hr]
    s_up = [_dot_tn(ks[h] * egl[:, h:h + 1], v_new[h]) for h in hr]
    for h in hr:
        state_ref[h] = s0[h] * egl_last[:, h:h + 1] + s_up[h]
        o = _rms(o_in[h] + o_at[h], ng_ref[...])
        z_h = z[:, h * dv:(h + 1) * dv]
        y_ref[0, :, h * dv:(h + 1) * dv] = (o * (z_h * _sigmoid(z_h))).astype(y_ref.dtype)

    @pl.when(n == pl.num_programs(1) - 1)
    def _():
        conv_ref[0] = tail
        s_ref[0] = state_ref[...]


def gdn_mixer(p, conv0, s0, prm, heads, dk, dv, conv_w):
    b, t, cols = p.shape
    c = min(CHUNK, t)
    cq = 2 * heads * dk + heads * dv
    pad = 8
    lanep = lambda a: jnp.pad(a.reshape(1, -1), ((0, 0), (0, LANE - a.size)))
    conv0p = jnp.pad(conv0, ((0, 0), (pad - (conv_w - 1), 0), (0, 0)))
    cw = jnp.pad(prm['conv_w'], ((0, pad - conv_w), (0, 0)))
    consts = [cw, lanep(prm['a_log']), lanep(prm['dt_bias']), prm['gdn_norm_g'].reshape(1, dv)]
    cspec = lambda a: pl.BlockSpec(a.shape, lambda i, j: (0, 0))
    y, conv, s = pl.pallas_call(
        functools.partial(_gdn_kernel, heads=heads, dk=dk, dv=dv, conv_w=conv_w),
        grid=(b, t // c),
        in_specs=[pl.BlockSpec((1, c, cols), lambda i, j: (i, j, 0)),
                  pl.BlockSpec((1, pad, cq), lambda i, j: (i, 0, 0)),
                  pl.BlockSpec((1, heads, dk, dv), lambda i, j: (i, 0, 0, 0))] + [cspec(a) for a in consts],
        out_specs=[pl.BlockSpec((1, c, heads * dv), lambda i, j: (i, j, 0)),
                   pl.BlockSpec((1, pad, cq), lambda i, j: (i, 0, 0)),
                   pl.BlockSpec((1, heads, dk, dv), lambda i, j: (i, 0, 0, 0))],
        out_shape=[jax.ShapeDtypeStruct((b, t, heads * dv), BF16),
                   jax.ShapeDtypeStruct((b, pad, cq), F32),
                   jax.ShapeDtypeStruct((b, heads, dk, dv), F32)],
        scratch_shapes=[pltpu.VMEM((c + pad, cq), F32),
                        pltpu.VMEM((heads, dk, dv), F32)],
        compiler_params=_cparams("parallel", "arbitrary"),
    )(p, conv0p, s0, *consts)
    return y, conv[:, pad - (conv_w - 1):, :], s


D_MODEL = 2048
DEPTH = 2
H_A, N_A = 12, 64
D_A = H_A * N_A
LORA_W, LORA_A, LORA_G = 64, 64, 128
H_B, DK_B, DV_B, LORA_GK = 4, 64, 128, 16
H_C, DK_C, DV_C, CONV_W = 6, 128, 128, 4
A_COLS = 3 * D_A + LORA_W + LORA_A + LORA_G
B_COLS = 2 * H_B * DK_B + 2 * H_B * DV_B + LORA_GK
C_QKV = 2 * H_C * DK_C + H_C * DV_C
C_COLS = C_QKV + H_C * DV_C + 2 * H_C
H_X = 4


def _row_tile(m, cap):
    t = min(m, cap)
    while m % t:
        t //= 2
    return t


def _col_tile(n, cap):
    if n <= 2 * cap:
        return n
    best = LANE
    for t in range(LANE, cap + 1, LANE):
        if n % t == 0:
            best = t
    return best


def _trunk(x, mem_k, mem_v, st_shift, st_rwkv, st_gla, st_conv, st_gdn, lw, final_g):
    b, t, d = x.shape
    m = b * t
    xf = x.reshape(m, d)
    tm = _row_tile(m, 1024)
    tm_s = _row_tile(m, 512)
    tq = _row_tile(t, 512)
    outs = [[] for _ in range(5)]
    for l in range(DEPTH):
        w = lw[l]
        proj = lambda wi: norm_matmul(xf, w['norm1_g'], wi, tm, _col_tile(wi.shape[1], 896)).reshape(b, t, -1)
        ya, sh, sa = rwkv_mixer(proj(w['w_in_a']), st_shift[l], st_rwkv[l], w, H_A, N_A, LORA_W, LORA_A)
        yb, sb = gla_mixer(proj(w['w_in_b']), st_gla[l], w, H_B, DK_B, DV_B, LORA_GK)
        yc, cb, sc = gdn_mixer(proj(w['w_in_c']), st_conv[l], st_gdn[l], w, H_C, DK_C, DV_C, CONV_W)
        x1 = out_proj(xf, ya.reshape(m, -1), yb.reshape(m, -1), yc.reshape(m, -1),
                      w['w_out_a'], w['w_out_b'], w['w_out_c'], tm_s)
        x2 = cross_attn(x1.reshape(b, t, d), w['normx_g'], w['wq_x'], w['wo_x'], mem_k[l], mem_v[l], H_X, tq)
        xf = mlp(x2.reshape(m, d), w['norm3_g'], final_g, w['w_up'], w['w_down'], tm_s, 512,
                 final_norm=(l == DEPTH - 1))
        for lst, val in zip(outs, (sh, sa, sb, cb, sc)):
            lst.append(val)
    return (xf.reshape(b, t, d),) + tuple(jnp.stack(o) for o in outs)


def kernel(x_prompt, x_sample, state_rwkv_shift, state_rwkv, state_gla, state_gdn_conv, state_gdn, cache_mem_k, cache_mem_v, mem_prompt, norm1_g, w_in, mu_a, w0_a, w_w2, a0_a, w_a2, w_g2, k_k, k_a, r_k, lnx_g, lnx_b, w_gk2, b_gk, gla_norm_g, conv_w, a_log, dt_bias, gdn_norm_g, w_out, normx_g, normm_g, wq_x, wk_x, wv_x, wo_x, norm3_g, w_up, w_down, final_g):
    bf = lambda a: a.astype(BF16)
    w_in_a, w_in_b, w_in_c = split_cols(w_in, A_COLS, B_COLS, 256)
    lw = []
    for l in range(DEPTH):
        lw.append(dict(
            norm1_g=norm1_g[l],
            w_in_a=w_in_a[l], w_in_b=w_in_b[l], w_in_c=w_in_c[l],
            mu_a=mu_a[l], w0_a=w0_a[l], w_w2=w_w2[l], a0_a=a0_a[l], w_a2=w_a2[l], w_g2=w_g2[l],
            k_k=k_k[l], k_a=k_a[l], r_k=r_k[l], lnx_g=lnx_g[l], lnx_b=lnx_b[l],
            w_gk2=w_gk2[l], b_gk=b_gk[l], gla_norm_g=gla_norm_g[l],
            conv_w=conv_w[l], a_log=a_log[l], dt_bias=dt_bias[l], gdn_norm_g=gdn_norm_g[l],
            w_out_a=bf(w_out[l, :D_A]), w_out_b=bf(w_out[l, D_A:D_A + H_B * DV_B]),
            w_out_c=bf(w_out[l, D_A + H_B * DV_B:]),
            normx_g=normx_g[l], wq_x=bf(wq_x[l]), wo_x=bf(wo_x[l]),
            norm3_g=norm3_g[l], w_up=bf(w_up[l]), w_down=bf(w_down[l])))

    b, n_mem, d = mem_prompt.shape
    memf = mem_prompt.reshape(b * n_mem, d)
    hd_x = d // H_X
    tmm = _row_tile(b * n_mem, 1024)
    p_mem_k = jnp.stack([norm_matmul(memf, normm_g[l], bf(wk_x[l]), tmm, 512).reshape(b, n_mem, H_X, hd_x)
                         for l in range(DEPTH)])
    p_mem_v = jnp.stack([norm_matmul(memf, normm_g[l], bf(wv_x[l]), tmm, 512).reshape(b, n_mem, H_X, hd_x)
                         for l in range(DEPTH)])

    zeros = lambda *s: jnp.zeros((DEPTH, b) + s, F32)
    flat_mem = lambda a: bf(a.reshape(a.shape[0], a.shape[1], n_mem, d))
    y_prompt, p_shift, p_rwkv, p_gla, p_conv, p_gdn = _trunk(
        x_prompt, flat_mem(p_mem_k), flat_mem(p_mem_v),
        zeros(A_COLS), zeros(H_A, N_A, N_A), zeros(H_B, DK_B, DV_B), zeros(CONV_W - 1, C_QKV),
        zeros(H_C, DK_C, DV_C), lw, final_g)
    y_sample, s_shift, s_rwkv, s_gla, s_conv, s_gdn = _trunk(
        x_sample, flat_mem(cache_mem_k), flat_mem(cache_mem_v),
        state_rwkv_shift, state_rwkv, state_gla, state_gdn_conv, state_gdn, lw, final_g)
    return (y_prompt, y_sample, p_shift, p_rwkv, p_gla, p_conv, p_gdn, p_mem_k, p_mem_v,
            s_shift, s_rwkv, s_gla, s_conv, s_gdn)
```

```python
import functools

import jax
import jax.numpy as jnp
from jax import lax
from jax.experimental import pallas as pl
from jax.experimental.pallas import tpu as pltpu

F32 = jnp.float32
BF16 = jnp.bfloat16

EPS = 1e-6
RWKV_GN_EPS = 64e-5
GK_NORMALIZER = 16.0
CHUNK = 64
SUB = 16
LANE = 128
VMEM_LIMIT = 56 * 1024 * 1024


def _cparams(*sem):
    return pltpu.CompilerParams(dimension_semantics=sem, vmem_limit_bytes=VMEM_LIMIT)


MIN_CHAINS = 24


def _seq_block(b, heads):
    for d in range(1, b + 1):
        if b % d == 0 and d * heads >= MIN_CHAINS:
            return d
    return b


def _dot(a, b):
    return jnp.dot(a.astype(BF16), b.astype(BF16), preferred_element_type=F32)


def _dot_nt(a, b):
    return lax.dot_general(a.astype(BF16), b.astype(BF16), (((1,), (1,)), ((), ())),
                           preferred_element_type=F32)


def _dot_tn(a, b):
    return lax.dot_general(a.astype(BF16), b.astype(BF16), (((0,), (0,)), ((), ())),
                           preferred_element_type=F32)


def _split(x, parts):
    out = []
    r = x
    for _ in range(parts):
        p = r.astype(BF16)
        out.append(p)
        r = r - p.astype(F32)
    return out


def _dot_exact_lhs(l_bf, x, parts=3):
    acc = None
    for p in _split(x, parts):
        d = jnp.dot(l_bf, p, preferred_element_type=F32)
        acc = d if acc is None else acc + d
    return acc


def _dot_exact_rhs(x, r_bf, parts=2):
    acc = None
    for p in _split(x, parts):
        d = jnp.dot(p, r_bf, preferred_element_type=F32)
        acc = d if acc is None else acc + d
    return acc


def _rms(x, g):
    ms = jnp.mean(x * x, axis=-1, keepdims=True)
    return x * lax.rsqrt(ms + EPS) * g


def _sigmoid(x):
    return 1.0 / (1.0 + jnp.exp(-x))


def _softplus(x):
    return jnp.maximum(x, 0.0) + jnp.log(1.0 + jnp.exp(-jnp.abs(x)))


def _tri(c, strict):
    r = lax.broadcasted_iota(jnp.int32, (c, c), 0)
    q = lax.broadcasted_iota(jnp.int32, (c, c), 1)
    return (r > q) if strict else (r >= q)


INV_BASE = 8


def _unit_lower_solve(ls, rhss):
    c = ls[0].shape[0]
    r = lax.broadcasted_iota(jnp.int32, (c, c), 0)
    q = lax.broadcasted_iota(jnp.int32, (c, c), 1)
    eye = (r == q).astype(F32)
    base = min(INV_BASE, c)
    diag_blk = (r // base) == (q // base)
    ns = [jnp.where(diag_blk, -l, 0.0).astype(BF16) for l in ls]
    ts = [eye + n.astype(F32) for n in ns]
    p = 2
    while p < base:
        ns = [_dot(n, n).astype(BF16) for n in ns]
        ts = [t + _dot(t, n) for t, n in zip(ts, ns)]
        p *= 2
    s = base
    while s < c:
        sub_blk = ((r // s) % 2 == 1) & ((q // s) == (r // s) - 1)
        cs = [jnp.where(sub_blk, l, 0.0).astype(BF16) for l in ls]
        tb = [t.astype(BF16) for t in ts]
        cts = [_dot(cm, t).astype(BF16) for cm, t in zip(cs, tb)]
        ts = [t - _dot(b, ct) for t, b, ct in zip(ts, tb, cts)]
        s *= 2
    return [_dot(t, rh) for t, rh in zip(ts, rhss)]


def _norm_matmul_kernel(x_ref, g_ref, w_ref, o_ref, h_ref):
    @pl.when(pl.program_id(1) == 0)
    def _():
        h_ref[...] = _rms(x_ref[...], g_ref[...]).astype(BF16)

    o_ref[...] = jnp.dot(h_ref[...], w_ref[...], preferred_element_type=F32).astype(o_ref.dtype)


def norm_matmul(x, g, w, tm, tn):
    m, k = x.shape
    n = w.shape[1]
    return pl.pallas_call(
        _norm_matmul_kernel,
        grid=(m // tm, n // tn),
        in_specs=[pl.BlockSpec((tm, k), lambda i, j: (i, 0)),
                  pl.BlockSpec((1, k), lambda i, j: (0, 0)),
                  pl.BlockSpec((k, tn), lambda i, j: (0, j))],
        out_specs=pl.BlockSpec((tm, tn), lambda i, j: (i, j)),
        out_shape=jax.ShapeDtypeStruct((m, n), F32),
        scratch_shapes=[pltpu.VMEM((tm, k), BF16)],
        compiler_params=_cparams("parallel", "arbitrary"),
    )(x, g.reshape(1, k), w)


def _split_cols_kernel(w_ref, a_ref, b_ref, c_ref, *, na, nb):
    n = w_ref.shape[2]
    nc = n - na - nb
    a_ref[0] = w_ref[0, :, 0:na].astype(BF16)
    b_ref[0, :, 0:nb] = w_ref[0, :, na:na + nb].astype(BF16)
    b_ref[0, :, nb:] = jnp.zeros((b_ref.shape[1], b_ref.shape[2] - nb), BF16)
    c_ref[0, :, 0:nc] = w_ref[0, :, na + nb:].astype(BF16)
    c_ref[0, :, nc:] = jnp.zeros((c_ref.shape[1], c_ref.shape[2] - nc), BF16)


def split_cols(w, na, nb, tk):
    depth, k, n = w.shape
    nc = n - na - nb
    widths = [na, nb + (-nb % LANE), nc + (-nc % LANE)]
    return pl.pallas_call(
        functools.partial(_split_cols_kernel, na=na, nb=nb),
        grid=(depth, k // tk),
        in_specs=[pl.BlockSpec((1, tk, n), lambda l, i: (l, i, 0))],
        out_specs=[pl.BlockSpec((1, tk, wd), lambda l, i: (l, i, 0)) for wd in widths],
        out_shape=[jax.ShapeDtypeStruct((depth, k, wd), BF16) for wd in widths],
        compiler_params=_cparams("parallel", "parallel"),
    )(w)


def _out_proj_kernel(x_ref, ya_ref, yb_ref, yc_ref, wa_ref, wb_ref, wc_ref, o_ref):
    acc = jnp.dot(ya_ref[...], wa_ref[...], preferred_element_type=F32)
    acc += jnp.dot(yb_ref[...], wb_ref[...], preferred_element_type=F32)
    acc += jnp.dot(yc_ref[...], wc_ref[...], preferred_element_type=F32)
    o_ref[...] = x_ref[...] + acc


def out_proj(x, ya, yb, yc, wa, wb, wc, tm):
    m, d = x.shape
    row = lambda a: pl.BlockSpec((tm, a.shape[1]), lambda i: (i, 0))
    full = lambda a: pl.BlockSpec(a.shape, lambda i: (0, 0))
    return pl.pallas_call(
        _out_proj_kernel,
        grid=(m // tm,),
        in_specs=[row(x), row(ya), row(yb), row(yc), full(wa), full(wb), full(wc)],
        out_specs=row(x),
        out_shape=jax.ShapeDtypeStruct((m, d), F32),
        compiler_params=_cparams("parallel"),
    )(x, ya, yb, yc, wa, wb, wc)


def _cross_attn_kernel(x_ref, g_ref, wq_ref, wo_ref, k_ref, v_ref, o_ref, *, heads):
    x = x_ref[0]
    d = x.shape[-1]
    hd = d // heads
    h = _rms(x, g_ref[...]).astype(BF16)
    q = jnp.dot(h, wq_ref[...], preferred_element_type=F32).astype(BF16)
    outs = []
    for i in range(heads):
        hs = slice(i * hd, (i + 1) * hd)
        s = _dot_nt(q[:, hs], k_ref[0, :, hs]) * (hd ** -0.5)
        s = s - jnp.max(s, axis=-1, keepdims=True)
        e = jnp.exp(s)
        pr = e / jnp.sum(e, axis=-1, keepdims=True)
        outs.append(jnp.dot(pr.astype(BF16), v_ref[0, :, hs], preferred_element_type=F32).astype(BF16))
    o = jnp.concatenate(outs, axis=-1)
    o_ref[0] = x + jnp.dot(o, wo_ref[...], preferred_element_type=F32)


def cross_attn(x, g, wq, wo, mk, mv, heads, tq):
    b, t, d = x.shape
    n_mem = mk.shape[1]
    full = lambda a: pl.BlockSpec(a.shape, lambda i, j: (0, 0))
    return pl.pallas_call(
        functools.partial(_cross_attn_kernel, heads=heads),
        grid=(b, t // tq),
        in_specs=[pl.BlockSpec((1, tq, d), lambda i, j: (i, j, 0)),
                  pl.BlockSpec((1, d), lambda i, j: (0, 0)),
                  full(wq), full(wo),
                  pl.BlockSpec((1, n_mem, d), lambda i, j: (i, 0, 0)),
                  pl.BlockSpec((1, n_mem, d), lambda i, j: (i, 0, 0))],
        out_specs=pl.BlockSpec((1, tq, d), lambda i, j: (i, j, 0)),
        out_shape=jax.ShapeDtypeStruct((b, t, d), F32),
        compiler_params=_cparams("parallel", "parallel"),
    )(x, g.reshape(1, d), wq, wo, mk, mv)


def _mlp_kernel(x_ref, g_ref, fg_ref, wu_ref, wd_ref, o_ref, h_ref, acc_ref, *, final_norm):
    f = pl.program_id(1)

    @pl.when(f == 0)
    def _():
        h_ref[...] = _rms(x_ref[...], g_ref[...]).astype(BF16)
        acc_ref[...] = jnp.zeros_like(acc_ref)

    u = jnp.maximum(jnp.dot(h_ref[...], wu_ref[...], preferred_element_type=F32), 0.0)
    acc_ref[...] += jnp.dot((u * u).astype(BF16), wd_ref[...], preferred_element_type=F32)

    @pl.when(f == pl.num_programs(1) - 1)
    def _():
        y = x_ref[...] + acc_ref[...]
        if final_norm:
            y = _rms(y, fg_ref[...])
        o_ref[...] = y


def mlp(x, g, fg, wu, wd, tm, tf, final_norm):
    m, d = x.shape
    dff = wu.shape[1]
    return pl.pallas_call(
        functools.partial(_mlp_kernel, final_norm=final_norm),
        grid=(m // tm, dff // tf),
        in_specs=[pl.BlockSpec((tm, d), lambda i, j: (i, 0)),
                  pl.BlockSpec((1, d), lambda i, j: (0, 0)),
                  pl.BlockSpec((1, d), lambda i, j: (0, 0)),
                  pl.BlockSpec((d, tf), lambda i, j: (0, j)),
                  pl.BlockSpec((tf, d), lambda i, j: (j, 0))],
        out_specs=pl.BlockSpec((tm, d), lambda i, j: (i, 0)),
        out_shape=jax.ShapeDtypeStruct((m, d), F32),
        scratch_shapes=[pltpu.VMEM((tm, d), BF16), pltpu.VMEM((tm, d), F32)],
        compiler_params=_cparams("parallel", "arbitrary"),
    )(x, g.reshape(1, d), fg.reshape(1, d), wu, wd)


def _rwkv_kernel(p_ref, shift0_ref, s0_ref, mu_ref, w0_ref, ww2_ref, a0_ref, wa2_ref, wg2_ref,
                 kk_ref, ka_ref, rk_ref, lng_ref, lnb_ref, seg_ref, segt_ref,
                 y_ref, shift_ref, s_ref,
                 carry_ref, state_ref, fe_ref, yh_ref, *, heads, hd, lora_w, lora_a):
    n = pl.program_id(1)
    bb, c, _ = p_ref.shape
    da = heads * hd

    @pl.when(n == 0)
    def _():
        carry_ref[...] = shift0_ref[...]
        state_ref[...] = s0_ref[...]

    seg = seg_ref[...]
    segt = segt_ref[...]
    seg_sum = lambda t: _head_sum(t, seg, segt)
    tril_bf = _tri(c, False).astype(BF16)

    egls = []
    for bi in range(bb):
        pf = p_ref[bi]
        row = lax.broadcasted_iota(jnp.int32, pf.shape, 0)
        prev = jnp.where(row == 0, carry_ref[bi], pltpu.roll(pf, shift=1, axis=0))
        carry_ref[bi] = pf[c - 1:c, :]
        ps = pf + (prev - pf) * mu_ref[...]

        r = ps[:, 0:da]
        k = ps[:, da:2 * da]
        v = ps[:, 2 * da:3 * da]
        o = 3 * da
        xw = ps[:, o:o + lora_w]
        xa = ps[:, o + lora_w:o + lora_w + lora_a]
        xg = ps[:, o + lora_w + lora_a:]

        w_log = -_softplus(-(w0_ref[...] + _dot(jnp.tanh(xw), ww2_ref[...]))) - 0.5
        ld = -jnp.exp(w_log)
        a = _sigmoid(a0_ref[...] + _dot(xa, wa2_ref[...]))
        fe_ref[bi, 8] = _dot(_sigmoid(xg), wg2_ref[...])

        kraw = k * kk_ref[...]
        kk = kraw * lax.rsqrt(seg_sum(kraw * kraw) + EPS)
        k2 = k * (1.0 + (a - 1.0) * ka_ref[...])
        b = kk * a

        gam = _dot_exact_lhs(tril_bf, ld)
        gl = gam[c - 1:c, :]
        eng = jnp.exp(-gam)
        el = jnp.exp(gl - gam)
        fe_ref[bi, 0] = kk * jnp.exp(gam - ld)
        fe_ref[bi, 1] = r * jnp.exp(gam)
        fe_ref[bi, 2] = k2 * eng
        fe_ref[bi, 3] = b * eng
        fe_ref[bi, 4] = k2 * el
        fe_ref[bi, 5] = b * el
        fe_ref[bi, 6] = v
        fe_ref[bi, 7] = seg_sum(r * k2 * rk_ref[...]) * v
        egls.append(jnp.exp(gl))

    strict = _tri(c, True)
    incl = _tri(c, False)
    ch = [(bi, h, slice(h * hd, (h + 1) * hd)) for bi in range(bb) for h in range(heads)]
    cr = range(len(ch))
    lhs = [jnp.concatenate([fe_ref[bi, 0, :, s], fe_ref[bi, 1, :, s]], axis=0).astype(BF16) for bi, _, s in ch]
    vh = [fe_ref[bi, 6, :, s].astype(BF16) for bi, _, s in ch]
    s0 = [state_ref[bi, h] for bi, h, _ in ch]
    sk = [_dot_nt(lhs[i], fe_ref[ch[i][0], 2, :, ch[i][2]]) for i in cr]
    sb = [_dot_nt(lhs[i], fe_ref[ch[i][0], 3, :, ch[i][2]]) for i in cr]
    pa = [_dot_nt(lhs[i], s0[i]) for i in cr]
    rhs = [pa[i][:c] + _dot(jnp.where(strict, sk[i][:c], 0.0), vh[i]) for i in cr]
    u = _unit_lower_solve([jnp.where(strict, sb[i][:c], 0.0) for i in cr], rhs)
    ya = [_dot(jnp.where(incl, sk[i][c:], 0.0), vh[i]) for i in cr]
    yb = [_dot(jnp.where(incl, sb[i][c:], 0.0), u[i]) for i in cr]
    sa = [_dot_tn(vh[i], fe_ref[ch[i][0], 4, :, ch[i][2]]) for i in cr]
    sc = [_dot_tn(u[i], fe_ref[ch[i][0], 5, :, ch[i][2]]) for i in cr]
    for i in cr:
        bi, h, s = ch[i]
        yh_ref[bi, :, s] = pa[i][c:] + ya[i] - yb[i]
        state_ref[bi, h] = s0[i] * egls[bi][:, s] + sa[i] - sc[i]

    inv_n = 1.0 / hd
    for bi in range(bb):
        y = yh_ref[bi]
        mean = seg_sum(y) * inv_n
        dlt = y - mean
        var = seg_sum(dlt * dlt) * inv_n
        y = dlt * lax.rsqrt(var + RWKV_GN_EPS) * lng_ref[...] + lnb_ref[...]
        y_ref[bi] = ((y + fe_ref[bi, 7]) * fe_ref[bi, 8]).astype(y_ref.dtype)

    @pl.when(n == pl.num_programs(1) - 1)
    def _():
        shift_ref[...] = carry_ref[...]
        s_ref[...] = state_ref[...]


def _head_sum(t, seg, segt):
    return _dot_exact_rhs(_dot_exact_rhs(t, seg), segt)


def rwkv_mixer(p, shift0, s0, prm, heads, hd, lora_w, lora_a):
    b, t, cols = p.shape
    c = min(CHUNK, t)
    da = heads * hd
    seg = (jnp.arange(da)[:, None] // hd == jnp.arange(LANE)[None, :]).astype(BF16)
    segt = seg.T
    rowp = lambda a: a.reshape(1, -1)
    consts = [rowp(prm['mu_a']), rowp(prm['w0_a']), prm['w_w2'], rowp(prm['a0_a']), prm['w_a2'], prm['w_g2'],
              rowp(prm['k_k']), rowp(prm['k_a']), rowp(prm['r_k']), rowp(prm['lnx_g']), rowp(prm['lnx_b']),
              seg, segt]
    cspec = lambda a: pl.BlockSpec(a.shape, lambda i, j: (0, 0))
    bb = _seq_block(b, heads)
    y, shift, s = pl.pallas_call(
        functools.partial(_rwkv_kernel, heads=heads, hd=hd, lora_w=lora_w, lora_a=lora_a),
        grid=(b // bb, t // c),
        in_specs=[pl.BlockSpec((bb, c, cols), lambda i, j: (i, j, 0)),
                  pl.BlockSpec((bb, 1, cols), lambda i, j: (i, 0, 0)),
                  pl.BlockSpec((bb, heads, hd, hd), lambda i, j: (i, 0, 0, 0))] + [cspec(a) for a in consts],
        out_specs=[pl.BlockSpec((bb, c, da), lambda i, j: (i, j, 0)),
                   pl.BlockSpec((bb, 1, cols), lambda i, j: (i, 0, 0)),
                   pl.BlockSpec((bb, heads, hd, hd), lambda i, j: (i, 0, 0, 0))],
        out_shape=[jax.ShapeDtypeStruct((b, t, da), BF16),
                   jax.ShapeDtypeStruct((b, 1, cols), F32),
                   jax.ShapeDtypeStruct((b, heads, hd, hd), F32)],
        scratch_shapes=[pltpu.VMEM((bb, 1, cols), F32),
                        pltpu.VMEM((bb, heads, hd, hd), F32),
                        pltpu.VMEM((bb, 9, c, da), F32),
                        pltpu.VMEM((bb, c, da), F32)],
        compiler_params=_cparams("parallel", "arbitrary"),
    )(p, shift0.reshape(b, 1, cols), s0, *consts)
    return y, shift.reshape(b, cols), s


def _gla_kernel(p_ref, s0_ref, wgk_ref, bgk_ref, ng_ref, wseg_ref,
                y_ref, s_ref,
                state_ref, *, heads, dk, dv, lora):
    n = pl.program_id(1)
    bb, c, _ = p_ref.shape
    hk = heads * dk
    hv = heads * dv
    sub = min(SUB, c)
    nb = c // sub

    @pl.when(n == 0)
    def _():
        state_ref[...] = s0_ref[...]

    hr = range(heads)
    sr = range(bb)
    ksl = [slice(h * dk, (h + 1) * dk) for h in hr]
    vsl = [slice(h * dv, (h + 1) * dv) for h in hr]
    srow = lax.broadcasted_iota(jnp.int32, (sub, hk), 0)
    tril_bf = _tri(c, False).astype(BF16)
    wseg = wseg_ref[...]

    vs, gs, vbs, qgs, kds, egls, xss, qts, kts = [], [], [], [], [], [], [], [], []
    for si in sr:
        pf = p_ref[si]
        q = pf[:, 0:hk] * (dk ** -0.5)
        k = pf[:, hk:2 * hk]
        v = pf[:, 2 * hk:2 * hk + hv]
        gk = pf[:, 2 * hk + 2 * hv:2 * hk + 2 * hv + lora]
        z = _dot(gk, wgk_ref[...]) + bgk_ref[...]
        glog = (jnp.minimum(z, 0.0) - jnp.log(1.0 + jnp.exp(-jnp.abs(z)))) * (1.0 / GK_NORMALIZER)
        gc = _dot_exact_lhs(tril_bf, glog)
        gl = gc[c - 1:c, :]
        vs.append(v)
        gs.append(pf[:, 2 * hk + hv:2 * hk + 2 * hv])
        vbs.append(v.astype(BF16))
        qgs.append((q * jnp.exp(gc)).astype(BF16))
        kds.append((k * jnp.exp(gl - gc)).astype(BF16))
        egls.append(jnp.exp(gl))
        xs_blocks = []
        for bi in range(nb):
            r0 = bi * sub
            q_i, gc_i, k_i = q[r0:r0 + sub], gc[r0:r0 + sub], k[r0:r0 + sub]
            xs = []
            for jj in range(sub):
                e = jnp.exp(jnp.minimum(gc_i - gc_i[jj:jj + 1], 0.0))
                xs.append(jnp.where(srow >= jj, q_i * e * k_i[jj:jj + 1], 0.0).astype(BF16))
            xs_blocks.append(jnp.concatenate(xs, axis=0))
        xss.append(xs_blocks)
        qt, kt = [None], [None]
        for bi in range(1, nb):
            r0 = bi * sub
            ref_row = gc[r0:r0 + 1]
            qt.append((q[r0:r0 + sub] * jnp.exp(gc[r0:r0 + sub] - ref_row)).astype(BF16))
            kt.append((k[:r0] * jnp.exp(jnp.minimum(ref_row - gc[:r0], 0.0))).astype(BF16))
        qts.append(qt)
        kts.append(kt)

    s0 = [[state_ref[si, h] for h in hr] for si in sr]
    reds = [[jnp.dot(xss[si][bi], wseg, preferred_element_type=F32) for bi in range(nb)] for si in sr]
    o_in = [[_dot_nt(qgs[si][:, ksl[h]], s0[si][h]) for h in hr] for si in sr]
    sc = [[[_dot_nt(qts[si][bi][:, ksl[h]], kts[si][bi][:, ksl[h]]).astype(BF16) for h in hr]
           for bi in range(1, nb)] for si in sr]
    off = [[[_dot(sc[si][bi - 1][h], vbs[si][:bi * sub, vsl[h]]) for h in hr]
            for bi in range(1, nb)] for si in sr]
    s_up = [[_dot_tn(vbs[si][:, vsl[h]], kds[si][:, ksl[h]]) for h in hr] for si in sr]

    for si in sr:
        blocks = []
        for bi in range(nb):
            r0 = bi * sub
            v_i = vs[si][r0:r0 + sub]
            red = reds[si][bi]
            o_i = red[0:sub] * v_i[0:1]
            for jj in range(1, sub):
                o_i = o_i + red[jj * sub:(jj + 1) * sub] * v_i[jj:jj + 1]
            if bi > 0:
                o_i = o_i + jnp.concatenate(off[si][bi - 1], axis=-1)
            blocks.append(o_i)
        o = (jnp.concatenate(blocks, axis=0) if nb > 1 else blocks[0]) + jnp.concatenate(o_in[si], axis=-1)
        for h in hr:
            state_ref[si, h] = s0[si][h] * egls[si][:, ksl[h]] + s_up[si][h]
            o_h = _rms(o[:, vsl[h]], ng_ref[...])
            g_h = gs[si][:, vsl[h]]
            y_ref[si, :, vsl[h]] = (o_h * (g_h * _sigmoid(g_h))).astype(y_ref.dtype)

    @pl.when(n == pl.num_programs(1) - 1)
    def _():
        s_ref[...] = state_ref[...]


def gla_mixer(p, s0, prm, heads, dk, dv, lora):
    b, t, cols = p.shape
    c = min(CHUNK, t)
    hk, hv = heads * dk, heads * dv
    wseg = (jnp.arange(hk)[:, None] // dk == jnp.arange(hv)[None, :] // dv).astype(BF16)
    consts = [prm['w_gk2'], prm['b_gk'].reshape(1, hk), prm['gla_norm_g'].reshape(1, dv), wseg]
    cspec = lambda a: pl.BlockSpec(a.shape, lambda i, j: (0, 0))
    bb = _seq_block(b, heads)
    y, s = pl.pallas_call(
        functools.partial(_gla_kernel, heads=heads, dk=dk, dv=dv, lora=lora),
        grid=(b // bb, t // c),
        in_specs=[pl.BlockSpec((bb, c, cols), lambda i, j: (i, j, 0)),
                  pl.BlockSpec((bb, heads, dv, dk), lambda i, j: (i, 0, 0, 0))] + [cspec(a) for a in consts],
        out_specs=[pl.BlockSpec((bb, c, hv), lambda i, j: (i, j, 0)),
                   pl.BlockSpec((bb, heads, dv, dk), lambda i, j: (i, 0, 0, 0))],
        out_shape=[jax.ShapeDtypeStruct((b, t, hv), BF16),
                   jax.ShapeDtypeStruct((b, heads, dv, dk), F32)],
        scratch_shapes=[pltpu.VMEM((bb, heads, dv, dk), F32)],
        compiler_params=_cparams("parallel", "arbitrary"),
    )(p, jnp.swapaxes(s0, -1, -2), *consts)
    return y, jnp.swapaxes(s, -1, -2)


def _gdn_kernel(p_ref, conv0_ref, s0_ref, cw_ref, alog_ref, dtb_ref, ng_ref,
                y_ref, conv_ref, s_ref,
                xp_ref, state_ref, *, heads, dk, dv, conv_w):
    n = pl.program_id(1)
    bb, c, _ = p_ref.shape
    cq = 2 * heads * dk + heads * dv
    pad = 8
    base = pad - (conv_w - 1)

    @pl.when(n == 0)
    def _():
        xp_ref[:, 0:pad, :] = conv0_ref[...]
        state_ref[...] = s0_ref[...]

    tril = _tri(c, False)
    strict = _tri(c, True)
    tril_bf = tril.astype(BF16)
    triu_bf = (lax.broadcasted_iota(jnp.int32, (c, c), 0) <= lax.broadcasted_iota(jnp.int32, (c, c), 1)).astype(BF16)

    ch = [(si, h) for si in range(bb) for h in range(heads)]
    cr = range(len(ch))
    qs, qgs, ks, kds, kbs, rhss, decs, egl_last = [], [], [], [], [], [], [], []
    for si in range(bb):
        xp_ref[si, pad:pad + c, :] = p_ref[si, :, 0:cq]
        conv = xp_ref[si, base:base + c, :] * cw_ref[0:1, :]
        for w in range(1, conv_w):
            conv = conv + xp_ref[si, base + w:base + w + c, :] * cw_ref[w:w + 1, :]
        xp_ref[si, 0:pad, :] = xp_ref[si, c:c + pad, :]
        qkv = conv * _sigmoid(conv)

        ab = p_ref[si, :, cq + heads * dv:]
        beta = _sigmoid(ab)
        g = -jnp.exp(alog_ref[...]) * _softplus(ab + dtb_ref[...])
        gc = _dot_exact_lhs(tril_bf, g)
        gct = None
        for part in _split(g, 3):
            d = lax.dot_general(part, triu_bf, (((0,), (0,)), ((), ())), preferred_element_type=F32)
            gct = d if gct is None else gct + d
        gl = gc[c - 1:c, :]
        eg = jnp.exp(gc)
        egl = jnp.exp(gl - gc)
        egl_last.append(jnp.exp(gl))
        for h in range(heads):
            q = qkv[:, h * dk:(h + 1) * dk]
            k = qkv[:, heads * dk + h * dk:heads * dk + (h + 1) * dk]
            v = qkv[:, 2 * heads * dk + h * dv:2 * heads * dk + (h + 1) * dv]
            q = q * lax.rsqrt(jnp.sum(q * q, axis=-1, keepdims=True) + EPS) * (dk ** -0.5)
            k = k * lax.rsqrt(jnp.sum(k * k, axis=-1, keepdims=True) + EPS)
            bcol = beta[:, heads + h:heads + h + 1]
            diff = gc[:, h:h + 1] - gct[h:h + 1, :]
            decs.append(jnp.where(tril, jnp.exp(jnp.where(tril, diff, 0.0)), 0.0))
            kb = k * bcol
            qs.append(q.astype(BF16))
            qgs.append((q * eg[:, h:h + 1]).astype(BF16))
            ks.append(k.astype(BF16))
            kds.append((k * egl[:, h:h + 1]).astype(BF16))
            kbs.append(kb.astype(BF16))
            rhss.append(jnp.concatenate([v * bcol, kb * eg[:, h:h + 1]], axis=-1))
    lmat = [jnp.where(strict, _dot_nt(kbs[i], ks[i]) * decs[i], 0.0) for i in cr]
    att = [(_dot_nt(qs[i], ks[i]) * decs[i]).astype(BF16) for i in cr]
    uw = _unit_lower_solve(lmat, rhss)
    s0 = [state_ref[si, h] for si, h in ch]
    s0b = [s.astype(BF16) for s in s0]
    v_new = [uw[i][:, :dv] - _dot(uw[i][:, dv:], s0b[i]) for i in cr]
    o_in = [_dot(qgs[i], s0b[i]) for i in cr]
    o_at = [_dot(att[i], v_new[i]) for i in cr]
    s_up = [_dot_tn(kds[i], v_new[i]) for i in cr]
    for i in cr:
        si, h = ch[i]
        state_ref[si, h] = s0[i] * egl_last[si][:, h:h + 1] + s_up[i]
        o = _rms(o_in[i] + o_at[i], ng_ref[...])
        z_h = p_ref[si, :, cq + h * dv:cq + (h + 1) * dv]
        y_ref[si, :, h * dv:(h + 1) * dv] = (o * (z_h * _sigmoid(z_h))).astype(y_ref.dtype)

    @pl.when(n == pl.num_programs(1) - 1)
    def _():
        conv_ref[...] = xp_ref[:, 0:pad, :]
        s_ref[...] = state_ref[...]


def gdn_mixer(p, conv0, s0, prm, heads, dk, dv, conv_w):
    b, t, cols = p.shape
    c = min(CHUNK, t)
    cq = 2 * heads * dk + heads * dv
    pad = 8
    lanep = lambda a: jnp.pad(a.reshape(1, -1), ((0, 0), (0, LANE - a.size)))
    conv0p = jnp.pad(conv0, ((0, 0), (pad - (conv_w - 1), 0), (0, 0)))
    cw = jnp.pad(prm['conv_w'], ((0, pad - conv_w), (0, 0)))
    consts = [cw, lanep(prm['a_log']), lanep(prm['dt_bias']), prm['gdn_norm_g'].reshape(1, dv)]
    cspec = lambda a: pl.BlockSpec(a.shape, lambda i, j: (0, 0))
    bb = _seq_block(b, heads)
    y, conv, s = pl.pallas_call(
        functools.partial(_gdn_kernel, heads=heads, dk=dk, dv=dv, conv_w=conv_w),
        grid=(b // bb, t // c),
        in_specs=[pl.BlockSpec((bb, c, cols), lambda i, j: (i, j, 0)),
                  pl.BlockSpec((bb, pad, cq), lambda i, j: (i, 0, 0)),
                  pl.BlockSpec((bb, heads, dk, dv), lambda i, j: (i, 0, 0, 0))] + [cspec(a) for a in consts],
        out_specs=[pl.BlockSpec((bb, c, heads * dv), lambda i, j: (i, j, 0)),
                   pl.BlockSpec((bb, pad, cq), lambda i, j: (i, 0, 0)),
                   pl.BlockSpec((bb, heads, dk, dv), lambda i, j: (i, 0, 0, 0))],
        out_shape=[jax.ShapeDtypeStruct((b, t, heads * dv), BF16),
                   jax.ShapeDtypeStruct((b, pad, cq), F32),
                   jax.ShapeDtypeStruct((b, heads, dk, dv), F32)],
        scratch_shapes=[pltpu.VMEM((bb, c + pad, cq), F32),
                        pltpu.VMEM((bb, heads, dk, dv), F32)],
        compiler_params=_cparams("parallel", "arbitrary"),
    )(p, conv0p, s0, *consts)
    return y, conv[:, pad - (conv_w - 1):, :], s


D_MODEL = 2048
DEPTH = 2
H_A, N_A = 12, 64
D_A = H_A * N_A
LORA_W, LORA_A, LORA_G = 64, 64, 128
H_B, DK_B, DV_B, LORA_GK = 4, 64, 128, 16
H_C, DK_C, DV_C, CONV_W = 6, 128, 128, 4
A_COLS = 3 * D_A + LORA_W + LORA_A + LORA_G
B_COLS = 2 * H_B * DK_B + 2 * H_B * DV_B + LORA_GK
C_QKV = 2 * H_C * DK_C + H_C * DV_C
C_COLS = C_QKV + H_C * DV_C + 2 * H_C
H_X = 4


def _row_tile(m, cap):
    t = min(m, cap)
    while m % t:
        t //= 2
    return t


def _col_tile(n, cap):
    if n <= 2 * cap:
        return n
    best = LANE
    for t in range(LANE, cap + 1, LANE):
        if n % t == 0:
            best = t
    return best


def _trunk(x, mem_k, mem_v, st_shift, st_rwkv, st_gla, st_conv, st_gdn, lw, final_g):
    b, t, d = x.shape
    m = b * t
    xf = x.reshape(m, d)
    tm = _row_tile(m, 1024)
    tm_s = _row_tile(m, 512)
    tq = _row_tile(t, 512)
    outs = [[] for _ in range(5)]
    for l in range(DEPTH):
        w = lw[l]
        proj = lambda wi: norm_matmul(xf, w['norm1_g'], wi, tm, _col_tile(wi.shape[1], 896)).reshape(b, t, -1)
        ya, sh, sa = rwkv_mixer(proj(w['w_in_a']), st_shift[l], st_rwkv[l], w, H_A, N_A, LORA_W, LORA_A)
        yb, sb = gla_mixer(proj(w['w_in_b']), st_gla[l], w, H_B, DK_B, DV_B, LORA_GK)
        yc, cb, sc = gdn_mixer(proj(w['w_in_c']), st_conv[l], st_gdn[l], w, H_C, DK_C, DV_C, CONV_W)
        x1 = out_proj(xf, ya.reshape(m, -1), yb.reshape(m, -1), yc.reshape(m, -1),
                      w['w_out_a'], w['w_out_b'], w['w_out_c'], tm_s)
        x2 = cross_attn(x1.reshape(b, t, d), w['normx_g'], w['wq_x'], w['wo_x'], mem_k[l], mem_v[l], H_X, tq)
        xf = mlp(x2.reshape(m, d), w['norm3_g'], final_g, w['w_up'], w['w_down'], tm_s, 512,
                 final_norm=(l == DEPTH - 1))
        for lst, val in zip(outs, (sh, sa, sb, cb, sc)):
            lst.append(val)
    return (xf.reshape(b, t, d),) + tuple(jnp.stack(o) for o in outs)


def kernel(x_prompt, x_sample, state_rwkv_shift, state_rwkv, state_gla, state_gdn_conv, state_gdn, cache_mem_k, cache_mem_v, mem_prompt, norm1_g, w_in, mu_a, w0_a, w_w2, a0_a, w_a2, w_g2, k_k, k_a, r_k, lnx_g, lnx_b, w_gk2, b_gk, gla_norm_g, conv_w, a_log, dt_bias, gdn_norm_g, w_out, normx_g, normm_g, wq_x, wk_x, wv_x, wo_x, norm3_g, w_up, w_down, final_g):
    bf = lambda a: a.astype(BF16)
    w_in_a, w_in_b, w_in_c = split_cols(w_in, A_COLS, B_COLS, 256)
    lw = []
    for l in range(DEPTH):
        lw.append(dict(
            norm1_g=norm1_g[l],
            w_in_a=w_in_a[l], w_in_b=w_in_b[l], w_in_c=w_in_c[l],
            mu_a=mu_a[l], w0_a=w0_a[l], w_w2=w_w2[l], a0_a=a0_a[l], w_a2=w_a2[l], w_g2=w_g2[l],
            k_k=k_k[l], k_a=k_a[l], r_k=r_k[l], lnx_g=lnx_g[l], lnx_b=lnx_b[l],
            w_gk2=w_gk2[l], b_gk=b_gk[l], gla_norm_g=gla_norm_g[l],
            conv_w=conv_w[l], a_log=a_log[l], dt_bias=dt_bias[l], gdn_norm_g=gdn_norm_g[l],
            w_out_a=bf(w_out[l, :D_A]), w_out_b=bf(w_out[l, D_A:D_A + H_B * DV_B]),
            w_out_c=bf(w_out[l, D_A + H_B * DV_B:]),
            normx_g=normx_g[l], wq_x=bf(wq_x[l]), wo_x=bf(wo_x[l]),
            norm3_g=norm3_g[l], w_up=bf(w_up[l]), w_down=bf(w_down[l])))

    b, n_mem, d = mem_prompt.shape
    memf = mem_prompt.reshape(b * n_mem, d)
    hd_x = d // H_X
    tmm = _row_tile(b * n_mem, 1024)
    p_mem_k = jnp.stack([norm_matmul(memf, normm_g[l], bf(wk_x[l]), tmm, 512).reshape(b, n_mem, H_X, hd_x)
                         for l in range(DEPTH)])
    p_mem_v = jnp.stack([norm_matmul(memf, normm_g[l], bf(wv_x[l]), tmm, 512).reshape(b, n_mem, H_X, hd_x)
                         for l in range(DEPTH)])

    zeros = lambda *s: jnp.zeros((DEPTH, b) + s, F32)
    flat_mem = lambda a: bf(a.reshape(a.shape[0], a.shape[1], n_mem, d))
    y_prompt, p_shift, p_rwkv, p_gla, p_conv, p_gdn = _trunk(
        x_prompt, flat_mem(p_mem_k), flat_mem(p_mem_v),
        zeros(A_COLS), zeros(H_A, N_A, N_A), zeros(H_B, DK_B, DV_B), zeros(CONV_W - 1, C_QKV),
        zeros(H_C, DK_C, DV_C), lw, final_g)
    y_sample, s_shift, s_rwkv, s_gla, s_conv, s_gdn = _trunk(
        x_sample, flat_mem(cache_mem_k), flat_mem(cache_mem_v),
        state_rwkv_shift, state_rwkv, state_gla, state_gdn_conv, state_gdn, lw, final_g)
    return (y_prompt, y_sample, p_shift, p_rwkv, p_gla, p_conv, p_gdn, p_mem_k, p_mem_v,
            s_shift, s_rwkv, s_gla, s_conv, s_gdn)
```

```python
import functools

import jax
import jax.numpy as jnp
from jax import lax
from jax.experimental import pallas as pl
from jax.experimental.pallas import tpu as pltpu

F32 = jnp.float32
BF16 = jnp.bfloat16

EPS = 1e-6
RWKV_GN_EPS = 64e-5
GK_NORMALIZER = 16.0
CHUNK = 64
SUB = 16
LANE = 128
VMEM_LIMIT = 56 * 1024 * 1024


def _cparams(*sem):
    return pltpu.CompilerParams(dimension_semantics=sem, vmem_limit_bytes=VMEM_LIMIT)


MIN_CHAINS = 24


def _seq_block(b, heads):
    for d in range(1, b + 1):
        if b % d == 0 and d * heads >= MIN_CHAINS:
            return d
    return b


def _seq_groups(b, heads):
    g = _seq_block(b, heads)
    return g, (tuple(range(g)),)


def _dot(a, b):
    return jnp.dot(a.astype(BF16), b.astype(BF16), preferred_element_type=F32)


def _dot_nt(a, b):
    return lax.dot_general(a.astype(BF16), b.astype(BF16), (((1,), (1,)), ((), ())),
                           preferred_element_type=F32)


def _dot_tn(a, b):
    return lax.dot_general(a.astype(BF16), b.astype(BF16), (((0,), (0,)), ((), ())),
                           preferred_element_type=F32)


def _split(x, parts):
    out = []
    r = x
    for _ in range(parts):
        p = r.astype(BF16)
        out.append(p)
        r = r - p.astype(F32)
    return out


def _dot_exact_lhs(l_bf, x, parts=3):
    acc = None
    for p in _split(x, parts):
        d = jnp.dot(l_bf, p, preferred_element_type=F32)
        acc = d if acc is None else acc + d
    return acc


def _dot_exact_rhs(x, r_bf, parts=2):
    acc = None
    for p in _split(x, parts):
        d = jnp.dot(p, r_bf, preferred_element_type=F32)
        acc = d if acc is None else acc + d
    return acc


def _rms(x, g):
    ms = jnp.mean(x * x, axis=-1, keepdims=True)
    return x * lax.rsqrt(ms + EPS) * g


def _sigmoid(x):
    return 1.0 / (1.0 + jnp.exp(-x))


def _softplus(x):
    return jnp.maximum(x, 0.0) + jnp.log(1.0 + jnp.exp(-jnp.abs(x)))


def _tri(c, strict):
    r = lax.broadcasted_iota(jnp.int32, (c, c), 0)
    q = lax.broadcasted_iota(jnp.int32, (c, c), 1)
    return (r > q) if strict else (r >= q)


INV_BASE = 8


def _unit_lower_solve(ls, rhss):
    return _drain(_unit_lower_solve_stages(ls, rhss))


def _unit_lower_solve_stages(ls, rhss):
    c = ls[0].shape[0]
    r = lax.broadcasted_iota(jnp.int32, (c, c), 0)
    q = lax.broadcasted_iota(jnp.int32, (c, c), 1)
    eye = (r == q).astype(F32)
    base = min(INV_BASE, c)
    diag_blk = (r // base) == (q // base)
    ns = [jnp.where(diag_blk, -l, 0.0).astype(BF16) for l in ls]
    ts = [eye + n.astype(F32) for n in ns]
    p = 2
    while p < base:
        ns = [_dot(n, n).astype(BF16) for n in ns]
        yield
        ts = [t + _dot(t, n) for t, n in zip(ts, ns)]
        yield
        p *= 2
    s = base
    while s < c:
        sub_blk = ((r // s) % 2 == 1) & ((q // s) == (r // s) - 1)
        cs = [jnp.where(sub_blk, l, 0.0).astype(BF16) for l in ls]
        tb = [t.astype(BF16) for t in ts]
        cts = [_dot(cm, t).astype(BF16) for cm, t in zip(cs, tb)]
        yield
        ts = [t - _dot(b, ct) for t, b, ct in zip(ts, tb, cts)]
        yield
        s *= 2
    return [_dot(t, rh) for t, rh in zip(ts, rhss)]


def _drain(gen):
    try:
        while True:
            next(gen)
    except StopIteration as stop:
        return stop.value


def _interleave(*gens):
    live = list(gens)
    while live:
        for g in list(live):
            try:
                next(g)
            except StopIteration:
                live.remove(g)


def _norm_matmul_kernel(x_ref, g_ref, w_ref, o_ref, h_ref):
    @pl.when(pl.program_id(1) == 0)
    def _():
        h_ref[...] = _rms(x_ref[...], g_ref[...]).astype(BF16)

    o_ref[...] = jnp.dot(h_ref[...], w_ref[...], preferred_element_type=F32).astype(o_ref.dtype)


def norm_matmul(x, g, w, tm, tn):
    m, k = x.shape
    n = w.shape[1]
    return pl.pallas_call(
        _norm_matmul_kernel,
        grid=(m // tm, n // tn),
        in_specs=[pl.BlockSpec((tm, k), lambda i, j: (i, 0)),
                  pl.BlockSpec((1, k), lambda i, j: (0, 0)),
                  pl.BlockSpec((k, tn), lambda i, j: (0, j))],
        out_specs=pl.BlockSpec((tm, tn), lambda i, j: (i, j)),
        out_shape=jax.ShapeDtypeStruct((m, n), F32),
        scratch_shapes=[pltpu.VMEM((tm, k), BF16)],
        compiler_params=_cparams("parallel", "arbitrary"),
    )(x, g.reshape(1, k), w)


def _split_cols_kernel(w_ref, a_ref, b_ref, c_ref, *, na, nb):
    n = w_ref.shape[2]
    nc = n - na - nb
    a_ref[0] = w_ref[0, :, 0:na].astype(BF16)
    b_ref[0, :, 0:nb] = w_ref[0, :, na:na + nb].astype(BF16)
    b_ref[0, :, nb:] = jnp.zeros((b_ref.shape[1], b_ref.shape[2] - nb), BF16)
    c_ref[0, :, 0:nc] = w_ref[0, :, na + nb:].astype(BF16)
    c_ref[0, :, nc:] = jnp.zeros((c_ref.shape[1], c_ref.shape[2] - nc), BF16)


def split_cols(w, na, nb, tk):
    depth, k, n = w.shape
    nc = n - na - nb
    widths = [na, nb + (-nb % LANE), nc + (-nc % LANE)]
    return pl.pallas_call(
        functools.partial(_split_cols_kernel, na=na, nb=nb),
        grid=(depth, k // tk),
        in_specs=[pl.BlockSpec((1, tk, n), lambda l, i: (l, i, 0))],
        out_specs=[pl.BlockSpec((1, tk, wd), lambda l, i: (l, i, 0)) for wd in widths],
        out_shape=[jax.ShapeDtypeStruct((depth, k, wd), BF16) for wd in widths],
        compiler_params=_cparams("parallel", "parallel"),
    )(w)


def _out_proj_kernel(x_ref, ya_ref, yb_ref, yc_ref, wa_ref, wb_ref, wc_ref, o_ref):
    acc = jnp.dot(ya_ref[...], wa_ref[...], preferred_element_type=F32)
    acc += jnp.dot(yb_ref[...], wb_ref[...], preferred_element_type=F32)
    acc += jnp.dot(yc_ref[...], wc_ref[...], preferred_element_type=F32)
    o_ref[...] = x_ref[...] + acc


def out_proj(x, ya, yb, yc, wa, wb, wc, tm):
    m, d = x.shape
    row = lambda a: pl.BlockSpec((tm, a.shape[1]), lambda i: (i, 0))
    full = lambda a: pl.BlockSpec(a.shape, lambda i: (0, 0))
    return pl.pallas_call(
        _out_proj_kernel,
        grid=(m // tm,),
        in_specs=[row(x), row(ya), row(yb), row(yc), full(wa), full(wb), full(wc)],
        out_specs=row(x),
        out_shape=jax.ShapeDtypeStruct((m, d), F32),
        compiler_params=_cparams("parallel"),
    )(x, ya, yb, yc, wa, wb, wc)


def _cross_attn_kernel(x_ref, g_ref, wq_ref, wo_ref, k_ref, v_ref, o_ref, *, heads):
    x = x_ref[0]
    d = x.shape[-1]
    hd = d // heads
    h = _rms(x, g_ref[...]).astype(BF16)
    q = jnp.dot(h, wq_ref[...], preferred_element_type=F32).astype(BF16)
    outs = []
    for i in range(heads):
        hs = slice(i * hd, (i + 1) * hd)
        s = _dot_nt(q[:, hs], k_ref[0, :, hs]) * (hd ** -0.5)
        s = s - jnp.max(s, axis=-1, keepdims=True)
        e = jnp.exp(s)
        pr = e / jnp.sum(e, axis=-1, keepdims=True)
        outs.append(jnp.dot(pr.astype(BF16), v_ref[0, :, hs], preferred_element_type=F32).astype(BF16))
    o = jnp.concatenate(outs, axis=-1)
    o_ref[0] = x + jnp.dot(o, wo_ref[...], preferred_element_type=F32)


def cross_attn(x, g, wq, wo, mk, mv, heads, tq):
    b, t, d = x.shape
    n_mem = mk.shape[1]
    full = lambda a: pl.BlockSpec(a.shape, lambda i, j: (0, 0))
    return pl.pallas_call(
        functools.partial(_cross_attn_kernel, heads=heads),
        grid=(b, t // tq),
        in_specs=[pl.BlockSpec((1, tq, d), lambda i, j: (i, j, 0)),
                  pl.BlockSpec((1, d), lambda i, j: (0, 0)),
                  full(wq), full(wo),
                  pl.BlockSpec((1, n_mem, d), lambda i, j: (i, 0, 0)),
                  pl.BlockSpec((1, n_mem, d), lambda i, j: (i, 0, 0))],
        out_specs=pl.BlockSpec((1, tq, d), lambda i, j: (i, j, 0)),
        out_shape=jax.ShapeDtypeStruct((b, t, d), F32),
        compiler_params=_cparams("parallel", "parallel"),
    )(x, g.reshape(1, d), wq, wo, mk, mv)


def _mlp_kernel(x_ref, g_ref, fg_ref, wu_ref, wd_ref, o_ref, h_ref, acc_ref, *, final_norm):
    f = pl.program_id(1)

    @pl.when(f == 0)
    def _():
        h_ref[...] = _rms(x_ref[...], g_ref[...]).astype(BF16)
        acc_ref[...] = jnp.zeros_like(acc_ref)

    u = jnp.maximum(jnp.dot(h_ref[...], wu_ref[...], preferred_element_type=F32), 0.0)
    acc_ref[...] += jnp.dot((u * u).astype(BF16), wd_ref[...], preferred_element_type=F32)

    @pl.when(f == pl.num_programs(1) - 1)
    def _():
        y = x_ref[...] + acc_ref[...]
        if final_norm:
            y = _rms(y, fg_ref[...])
        o_ref[...] = y


def mlp(x, g, fg, wu, wd, tm, tf, final_norm):
    m, d = x.shape
    dff = wu.shape[1]
    return pl.pallas_call(
        functools.partial(_mlp_kernel, final_norm=final_norm),
        grid=(m // tm, dff // tf),
        in_specs=[pl.BlockSpec((tm, d), lambda i, j: (i, 0)),
                  pl.BlockSpec((1, d), lambda i, j: (0, 0)),
                  pl.BlockSpec((1, d), lambda i, j: (0, 0)),
                  pl.BlockSpec((d, tf), lambda i, j: (0, j)),
                  pl.BlockSpec((tf, d), lambda i, j: (j, 0))],
        out_specs=pl.BlockSpec((tm, d), lambda i, j: (i, 0)),
        out_shape=jax.ShapeDtypeStruct((m, d), F32),
        scratch_shapes=[pltpu.VMEM((tm, d), BF16), pltpu.VMEM((tm, d), F32)],
        compiler_params=_cparams("parallel", "arbitrary"),
    )(x, g.reshape(1, d), fg.reshape(1, d), wu, wd)


def _rwkv_kernel(p_ref, shift0_ref, s0_ref, *rest, groups, **dims):
    n = pl.program_id(1)
    scr = rest[-2 * len(groups):]
    base = rest[:-2 * len(groups)]
    shift_ref, s_ref, carry_ref, state_ref = base[-4:]

    @pl.when(n == 0)
    def _():
        carry_ref[...] = shift0_ref[...]
        state_ref[...] = s0_ref[...]

    egls = {}
    front = [_rwkv_front(seqs, egls, p_ref, *base, scr[2 * gi], scr[2 * gi + 1], **dims)
             for gi, seqs in enumerate(groups)]
    chain = [_rwkv_chain(seqs, egls, p_ref, *base, scr[2 * gi], scr[2 * gi + 1], **dims)
             for gi, seqs in enumerate(groups)]
    _drain(front[0])
    for gi in range(1, len(groups)):
        _interleave(chain[gi - 1], front[gi])
    _drain(chain[-1])

    @pl.when(n == pl.num_programs(1) - 1)
    def _():
        shift_ref[...] = carry_ref[...]
        s_ref[...] = state_ref[...]


def _rwkv_front(seqs, egls, p_ref, mu_ref, w0_ref, ww2_ref, a0_ref, wa2_ref, wg2_ref,
                kk_ref, ka_ref, rk_ref, lng_ref, lnb_ref, seg_ref, segt_ref,
                y_ref, shift_ref, s_ref,
                carry_ref, state_ref, fe_ref, yh_ref, *, heads, hd, lora_w, lora_a):
    c = p_ref.shape[1]
    da = heads * hd
    g0 = seqs[0]

    seg = seg_ref[...]
    segt = segt_ref[...]
    seg_sum = lambda t: _head_sum(t, seg, segt)
    tril_bf = _tri(c, False).astype(BF16)

    for bi in seqs:
        pf = p_ref[bi]
        row = lax.broadcasted_iota(jnp.int32, pf.shape, 0)
        prev = jnp.where(row == 0, carry_ref[bi], pltpu.roll(pf, shift=1, axis=0))
        carry_ref[bi] = pf[c - 1:c, :]
        ps = pf + (prev - pf) * mu_ref[...]

        r = ps[:, 0:da]
        k = ps[:, da:2 * da]
        v = ps[:, 2 * da:3 * da]
        o = 3 * da
        xw = ps[:, o:o + lora_w]
        xa = ps[:, o + lora_w:o + lora_w + lora_a]
        xg = ps[:, o + lora_w + lora_a:]

        w_log = -_softplus(-(w0_ref[...] + _dot(jnp.tanh(xw), ww2_ref[...]))) - 0.5
        ld = -jnp.exp(w_log)
        a = _sigmoid(a0_ref[...] + _dot(xa, wa2_ref[...]))
        fe_ref[bi - g0, 8] = _dot(_sigmoid(xg), wg2_ref[...])
        yield

        kraw = k * kk_ref[...]
        kk = kraw * lax.rsqrt(seg_sum(kraw * kraw) + EPS)
        k2 = k * (1.0 + (a - 1.0) * ka_ref[...])
        b = kk * a
        yield

        gam = _dot_exact_lhs(tril_bf, ld)
        gl = gam[c - 1:c, :]
        eng = jnp.exp(-gam)
        el = jnp.exp(gl - gam)
        fe_ref[bi - g0, 0] = kk * jnp.exp(gam - ld)
        fe_ref[bi - g0, 1] = r * jnp.exp(gam)
        fe_ref[bi - g0, 2] = k2 * eng
        fe_ref[bi - g0, 3] = b * eng
        yield
        fe_ref[bi - g0, 4] = k2 * el
        fe_ref[bi - g0, 5] = b * el
        fe_ref[bi - g0, 6] = v
        fe_ref[bi - g0, 7] = seg_sum(r * k2 * rk_ref[...]) * v
        egls[bi] = jnp.exp(gl)
        yield


def _rwkv_chain(seqs, egls, p_ref, mu_ref, w0_ref, ww2_ref, a0_ref, wa2_ref, wg2_ref,
                kk_ref, ka_ref, rk_ref, lng_ref, lnb_ref, seg_ref, segt_ref,
                y_ref, shift_ref, s_ref,
                carry_ref, state_ref, fe_ref, yh_ref, *, heads, hd, lora_w, lora_a):
    c = p_ref.shape[1]
    g0 = seqs[0]
    seg = seg_ref[...]
    segt = segt_ref[...]
    seg_sum = lambda t: _head_sum(t, seg, segt)
    strict = _tri(c, True)
    incl = _tri(c, False)
    ch = [(bi, h, slice(h * hd, (h + 1) * hd)) for bi in seqs for h in range(heads)]
    cr = range(len(ch))
    lhs = [jnp.concatenate([fe_ref[bi - g0, 0, :, s], fe_ref[bi - g0, 1, :, s]], axis=0).astype(BF16)
           for bi, _, s in ch]
    vh = [fe_ref[bi - g0, 6, :, s].astype(BF16) for bi, _, s in ch]
    s0 = [state_ref[bi, h] for bi, h, _ in ch]
    sk = [_dot_nt(lhs[i], fe_ref[ch[i][0] - g0, 2, :, ch[i][2]]) for i in cr]
    yield
    sb = [_dot_nt(lhs[i], fe_ref[ch[i][0] - g0, 3, :, ch[i][2]]) for i in cr]
    yield
    pa = [_dot_nt(lhs[i], s0[i]) for i in cr]
    yield
    rhs = [pa[i][:c] + _dot(jnp.where(strict, sk[i][:c], 0.0), vh[i]) for i in cr]
    yield
    u = yield from _unit_lower_solve_stages([jnp.where(strict, sb[i][:c], 0.0) for i in cr], rhs)
    yield
    ya = [_dot(jnp.where(incl, sk[i][c:], 0.0), vh[i]) for i in cr]
    yield
    yb = [_dot(jnp.where(incl, sb[i][c:], 0.0), u[i]) for i in cr]
    yield
    sa = [_dot_tn(vh[i], fe_ref[ch[i][0] - g0, 4, :, ch[i][2]]) for i in cr]
    yield
    sc = [_dot_tn(u[i], fe_ref[ch[i][0] - g0, 5, :, ch[i][2]]) for i in cr]
    yield
    for i in cr:
        bi, h, s = ch[i]
        yh_ref[bi - g0, :, s] = pa[i][c:] + ya[i] - yb[i]
        state_ref[bi, h] = s0[i] * egls[bi][:, s] + sa[i] - sc[i]
    yield

    inv_n = 1.0 / hd
    for bi in seqs:
        y = yh_ref[bi - g0]
        mean = seg_sum(y) * inv_n
        dlt = y - mean
        var = seg_sum(dlt * dlt) * inv_n
        y = dlt * lax.rsqrt(var + RWKV_GN_EPS) * lng_ref[...] + lnb_ref[...]
        y_ref[bi] = ((y + fe_ref[bi - g0, 7]) * fe_ref[bi - g0, 8]).astype(y_ref.dtype)
        yield


def _head_sum(t, seg, segt):
    return _dot_exact_rhs(_dot_exact_rhs(t, seg, parts=1), segt)


def rwkv_mixer(p, shift0, s0, prm, heads, hd, lora_w, lora_a):
    b, t, cols = p.shape
    c = min(CHUNK, t)
    da = heads * hd
    seg = (jnp.arange(da)[:, None] // hd == jnp.arange(LANE)[None, :]).astype(BF16)
    segt = seg.T
    rowp = lambda a: a.reshape(1, -1)
    consts = [rowp(prm['mu_a']), rowp(prm['w0_a']), prm['w_w2'], rowp(prm['a0_a']), prm['w_a2'], prm['w_g2'],
              rowp(prm['k_k']), rowp(prm['k_a']), rowp(prm['r_k']), rowp(prm['lnx_g']), rowp(prm['lnx_b']),
              seg, segt]
    cspec = lambda a: pl.BlockSpec(a.shape, lambda i, j: (0, 0))
    bb, groups = _seq_groups(b, heads)
    y, shift, s = pl.pallas_call(
        functools.partial(_rwkv_kernel, groups=groups, heads=heads, hd=hd, lora_w=lora_w, lora_a=lora_a),
        grid=(b // bb, t // c),
        in_specs=[pl.BlockSpec((bb, c, cols), lambda i, j: (i, j, 0)),
                  pl.BlockSpec((bb, 1, cols), lambda i, j: (i, 0, 0)),
                  pl.BlockSpec((bb, heads, hd, hd), lambda i, j: (i, 0, 0, 0))] + [cspec(a) for a in consts],
        out_specs=[pl.BlockSpec((bb, c, da), lambda i, j: (i, j, 0)),
                   pl.BlockSpec((bb, 1, cols), lambda i, j: (i, 0, 0)),
                   pl.BlockSpec((bb, heads, hd, hd), lambda i, j: (i, 0, 0, 0))],
        out_shape=[jax.ShapeDtypeStruct((b, t, da), BF16),
                   jax.ShapeDtypeStruct((b, 1, cols), F32),
                   jax.ShapeDtypeStruct((b, heads, hd, hd), F32)],
        scratch_shapes=[pltpu.VMEM((bb, 1, cols), F32),
                        pltpu.VMEM((bb, heads, hd, hd), F32)]
        + [pltpu.VMEM(shp, F32) for g in groups for shp in ((len(g), 9, c, da), (len(g), c, da))],
        compiler_params=_cparams("parallel", "arbitrary"),
    )(p, shift0.reshape(b, 1, cols), s0, *consts)
    return y, shift.reshape(b, cols), s


def _gla_kernel(p_ref, s0_ref, wgk_ref, bgk_ref, ng_ref, wseg_ref,
                y_ref, s_ref,
                state_ref, *, heads, dk, dv, lora):
    n = pl.program_id(1)
    bb, c, _ = p_ref.shape
    hk = heads * dk
    hv = heads * dv
    sub = min(SUB, c)
    nb = c // sub

    @pl.when(n == 0)
    def _():
        state_ref[...] = s0_ref[...]

    hr = range(heads)
    sr = range(bb)
    ksl = [slice(h * dk, (h + 1) * dk) for h in hr]
    vsl = [slice(h * dv, (h + 1) * dv) for h in hr]
    srow = lax.broadcasted_iota(jnp.int32, (sub, hk), 0)
    tril_bf = _tri(c, False).astype(BF16)
    wseg = wseg_ref[...]

    vs, gs, vbs, qgs, kds, egls, xss, qts, kts = [], [], [], [], [], [], [], [], []
    for si in sr:
        pf = p_ref[si]
        q = pf[:, 0:hk] * (dk ** -0.5)
        k = pf[:, hk:2 * hk]
        v = pf[:, 2 * hk:2 * hk + hv]
        gk = pf[:, 2 * hk + 2 * hv:2 * hk + 2 * hv + lora]
        z = _dot(gk, wgk_ref[...]) + bgk_ref[...]
        glog = (jnp.minimum(z, 0.0) - jnp.log(1.0 + jnp.exp(-jnp.abs(z)))) * (1.0 / GK_NORMALIZER)
        gc = _dot_exact_lhs(tril_bf, glog)
        gl = gc[c - 1:c, :]
        vs.append(v)
        gs.append(pf[:, 2 * hk + hv:2 * hk + 2 * hv])
        vbs.append(v.astype(BF16))
        qgs.append((q * jnp.exp(gc)).astype(BF16))
        kds.append((k * jnp.exp(gl - gc)).astype(BF16))
        egls.append(jnp.exp(gl))
        xs_blocks = []
        for bi in range(nb):
            r0 = bi * sub
            q_i, gc_i, k_i = q[r0:r0 + sub], gc[r0:r0 + sub], k[r0:r0 + sub]
            xs = []
            for jj in range(sub):
                e = jnp.exp(jnp.minimum(gc_i - gc_i[jj:jj + 1], 0.0))
                xs.append(jnp.where(srow >= jj, q_i * e * k_i[jj:jj + 1], 0.0).astype(BF16))
            xs_blocks.append(jnp.concatenate(xs, axis=0))
        xss.append(xs_blocks)
        qt, kt = [None], [None]
        for bi in range(1, nb):
            r0 = bi * sub
            ref_row = gc[r0:r0 + 1]
            qt.append((q[r0:r0 + sub] * jnp.exp(gc[r0:r0 + sub] - ref_row)).astype(BF16))
            kt.append((k[:r0] * jnp.exp(jnp.minimum(ref_row - gc[:r0], 0.0))).astype(BF16))
        qts.append(qt)
        kts.append(kt)

    s0 = [[state_ref[si, h] for h in hr] for si in sr]
    reds = [[jnp.dot(xss[si][bi], wseg, preferred_element_type=F32) for bi in range(nb)] for si in sr]
    o_in = [[_dot_nt(qgs[si][:, ksl[h]], s0[si][h]) for h in hr] for si in sr]
    sc = [[[_dot_nt(qts[si][bi][:, ksl[h]], kts[si][bi][:, ksl[h]]).astype(BF16) for h in hr]
           for bi in range(1, nb)] for si in sr]
    off = [[[_dot(sc[si][bi - 1][h], vbs[si][:bi * sub, vsl[h]]) for h in hr]
            for bi in range(1, nb)] for si in sr]
    s_up = [[_dot_tn(vbs[si][:, vsl[h]], kds[si][:, ksl[h]]) for h in hr] for si in sr]

    for si in sr:
        blocks = []
        for bi in range(nb):
            r0 = bi * sub
            v_i = vs[si][r0:r0 + sub]
            red = reds[si][bi]
            o_i = red[0:sub] * v_i[0:1]
            for jj in range(1, sub):
                o_i = o_i + red[jj * sub:(jj + 1) * sub] * v_i[jj:jj + 1]
            if bi > 0:
                o_i = o_i + jnp.concatenate(off[si][bi - 1], axis=-1)
            blocks.append(o_i)
        o = (jnp.concatenate(blocks, axis=0) if nb > 1 else blocks[0]) + jnp.concatenate(o_in[si], axis=-1)
        for h in hr:
            state_ref[si, h] = s0[si][h] * egls[si][:, ksl[h]] + s_up[si][h]
            o_h = _rms(o[:, vsl[h]], ng_ref[...])
            g_h = gs[si][:, vsl[h]]
            y_ref[si, :, vsl[h]] = (o_h * (g_h * _sigmoid(g_h))).astype(y_ref.dtype)

    @pl.when(n == pl.num_programs(1) - 1)
    def _():
        s_ref[...] = state_ref[...]


def gla_mixer(p, s0, prm, heads, dk, dv, lora):
    b, t, cols = p.shape
    c = min(CHUNK, t)
    hk, hv = heads * dk, heads * dv
    wseg = (jnp.arange(hk)[:, None] // dk == jnp.arange(hv)[None, :] // dv).astype(BF16)
    consts = [prm['w_gk2'], prm['b_gk'].reshape(1, hk), prm['gla_norm_g'].reshape(1, dv), wseg]
    cspec = lambda a: pl.BlockSpec(a.shape, lambda i, j: (0, 0))
    bb = _seq_block(b, heads)
    y, s = pl.pallas_call(
        functools.partial(_gla_kernel, heads=heads, dk=dk, dv=dv, lora=lora),
        grid=(b // bb, t // c),
        in_specs=[pl.BlockSpec((bb, c, cols), lambda i, j: (i, j, 0)),
                  pl.BlockSpec((bb, heads, dv, dk), lambda i, j: (i, 0, 0, 0))] + [cspec(a) for a in consts],
        out_specs=[pl.BlockSpec((bb, c, hv), lambda i, j: (i, j, 0)),
                   pl.BlockSpec((bb, heads, dv, dk), lambda i, j: (i, 0, 0, 0))],
        out_shape=[jax.ShapeDtypeStruct((b, t, hv), BF16),
                   jax.ShapeDtypeStruct((b, heads, dv, dk), F32)],
        scratch_shapes=[pltpu.VMEM((bb, heads, dv, dk), F32)],
        compiler_params=_cparams("parallel", "arbitrary"),
    )(p, jnp.swapaxes(s0, -1, -2), *consts)
    return y, jnp.swapaxes(s, -1, -2)


def _gdn_kernel(p_ref, conv0_ref, s0_ref, cw_ref, alog_ref, dtb_ref, ng_ref,
                y_ref, conv_ref, s_ref,
                xp_ref, state_ref, *, heads, dk, dv, conv_w):
    n = pl.program_id(1)
    bb, c, _ = p_ref.shape
    cq = 2 * heads * dk + heads * dv
    pad = 8
    base = pad - (conv_w - 1)

    @pl.when(n == 0)
    def _():
        xp_ref[:, 0:pad, :] = conv0_ref[...]
        state_ref[...] = s0_ref[...]

    tril = _tri(c, False)
    strict = _tri(c, True)
    tril_bf = tril.astype(BF16)
    triu_bf = (lax.broadcasted_iota(jnp.int32, (c, c), 0) <= lax.broadcasted_iota(jnp.int32, (c, c), 1)).astype(BF16)

    ch = [(si, h) for si in range(bb) for h in range(heads)]
    cr = range(len(ch))
    qs, qgs, ks, kds, kbs, rhss, decs, egl_last = [], [], [], [], [], [], [], []
    for si in range(bb):
        xp_ref[si, pad:pad + c, :] = p_ref[si, :, 0:cq]
        conv = xp_ref[si, base:base + c, :] * cw_ref[0:1, :]
        for w in range(1, conv_w):
            conv = conv + xp_ref[si, base + w:base + w + c, :] * cw_ref[w:w + 1, :]
        xp_ref[si, 0:pad, :] = xp_ref[si, c:c + pad, :]
        qkv = conv * _sigmoid(conv)

        ab = p_ref[si, :, cq + heads * dv:]
        beta = _sigmoid(ab)
        g = -jnp.exp(alog_ref[...]) * _softplus(ab + dtb_ref[...])
        gc = _dot_exact_lhs(tril_bf, g)
        gct = None
        for part in _split(g, 3):
            d = lax.dot_general(part, triu_bf, (((0,), (0,)), ((), ())), preferred_element_type=F32)
            gct = d if gct is None else gct + d
        gl = gc[c - 1:c, :]
        eg = jnp.exp(gc)
        egl = jnp.exp(gl - gc)
        egl_last.append(jnp.exp(gl))
        for h in range(heads):
            q = qkv[:, h * dk:(h + 1) * dk]
            k = qkv[:, heads * dk + h * dk:heads * dk + (h + 1) * dk]
            v = qkv[:, 2 * heads * dk + h * dv:2 * heads * dk + (h + 1) * dv]
            q = q * lax.rsqrt(jnp.sum(q * q, axis=-1, keepdims=True) + EPS) * (dk ** -0.5)
            k = k * lax.rsqrt(jnp.sum(k * k, axis=-1, keepdims=True) + EPS)
            bcol = beta[:, heads + h:heads + h + 1]
            diff = gc[:, h:h + 1] - gct[h:h + 1, :]
            decs.append(jnp.where(tril, jnp.exp(jnp.where(tril, diff, 0.0)), 0.0))
            kb = k * bcol
            qs.append(q.astype(BF16))
            qgs.append((q * eg[:, h:h + 1]).astype(BF16))
            ks.append(k.astype(BF16))
            kds.append((k * egl[:, h:h + 1]).astype(BF16))
            kbs.append(kb.astype(BF16))
            rhss.append(jnp.concatenate([v * bcol, kb * eg[:, h:h + 1]], axis=-1))
    lmat = [jnp.where(strict, _dot_nt(kbs[i], ks[i]) * decs[i], 0.0) for i in cr]
    att = [(_dot_nt(qs[i], ks[i]) * decs[i]).astype(BF16) for i in cr]
    uw = _unit_lower_solve(lmat, rhss)
    s0 = [state_ref[si, h] for si, h in ch]
    s0b = [s.astype(BF16) for s in s0]
    v_new = [uw[i][:, :dv] - _dot(uw[i][:, dv:], s0b[i]) for i in cr]
    o_in = [_dot(qgs[i], s0b[i]) for i in cr]
    o_at = [_dot(att[i], v_new[i]) for i in cr]
    s_up = [_dot_tn(kds[i], v_new[i]) for i in cr]
    for i in cr:
        si, h = ch[i]
        state_ref[si, h] = s0[i] * egl_last[si][:, h:h + 1] + s_up[i]
        o = _rms(o_in[i] + o_at[i], ng_ref[...])
        z_h = p_ref[si, :, cq + h * dv:cq + (h + 1) * dv]
        y_ref[si, :, h * dv:(h + 1) * dv] = (o * (z_h * _sigmoid(z_h))).astype(y_ref.dtype)

    @pl.when(n == pl.num_programs(1) - 1)
    def _():
        conv_ref[...] = xp_ref[:, 0:pad, :]
        s_ref[...] = state_ref[...]


def gdn_mixer(p, conv0, s0, prm, heads, dk, dv, conv_w):
    b, t, cols = p.shape
    c = min(CHUNK, t)
    cq = 2 * heads * dk + heads * dv
    pad = 8
    lanep = lambda a: jnp.pad(a.reshape(1, -1), ((0, 0), (0, LANE - a.size)))
    conv0p = jnp.pad(conv0, ((0, 0), (pad - (conv_w - 1), 0), (0, 0)))
    cw = jnp.pad(prm['conv_w'], ((0, pad - conv_w), (0, 0)))
    consts = [cw, lanep(prm['a_log']), lanep(prm['dt_bias']), prm['gdn_norm_g'].reshape(1, dv)]
    cspec = lambda a: pl.BlockSpec(a.shape, lambda i, j: (0, 0))
    bb = _seq_block(b, heads)
    y, conv, s = pl.pallas_call(
        functools.partial(_gdn_kernel, heads=heads, dk=dk, dv=dv, conv_w=conv_w),
        grid=(b // bb, t // c),
        in_specs=[pl.BlockSpec((bb, c, cols), lambda i, j: (i, j, 0)),
                  pl.BlockSpec((bb, pad, cq), lambda i, j: (i, 0, 0)),
                  pl.BlockSpec((bb, heads, dk, dv), lambda i, j: (i, 0, 0, 0))] + [cspec(a) for a in consts],
        out_specs=[pl.BlockSpec((bb, c, heads * dv), lambda i, j: (i, j, 0)),
                   pl.BlockSpec((bb, pad, cq), lambda i, j: (i, 0, 0)),
                   pl.BlockSpec((bb, heads, dk, dv), lambda i, j: (i, 0, 0, 0))],
        out_shape=[jax.ShapeDtypeStruct((b, t, heads * dv), BF16),
                   jax.ShapeDtypeStruct((b, pad, cq), F32),
                   jax.ShapeDtypeStruct((b, heads, dk, dv), F32)],
        scratch_shapes=[pltpu.VMEM((bb, c + pad, cq), F32),
                        pltpu.VMEM((bb, heads, dk, dv), F32)],
        compiler_params=_cparams("parallel", "arbitrary"),
    )(p, conv0p, s0, *consts)
    return y, conv[:, pad - (conv_w - 1):, :], s


D_MODEL = 2048
DEPTH = 2
H_A, N_A = 12, 64
D_A = H_A * N_A
LORA_W, LORA_A, LORA_G = 64, 64, 128
H_B, DK_B, DV_B, LORA_GK = 4, 64, 128, 16
H_C, DK_C, DV_C, CONV_W = 6, 128, 128, 4
A_COLS = 3 * D_A + LORA_W + LORA_A + LORA_G
B_COLS = 2 * H_B * DK_B + 2 * H_B * DV_B + LORA_GK
C_QKV = 2 * H_C * DK_C + H_C * DV_C
C_COLS = C_QKV + H_C * DV_C + 2 * H_C
H_X = 4


def _row_tile(m, cap):
    t = min(m, cap)
    while m % t:
        t //= 2
    return t


def _trunk(x, mem_k, mem_v, st_shift, st_rwkv, st_gla, st_conv, st_gdn, lw, final_g):
    b, t, d = x.shape
    m = b * t
    xf = x.reshape(m, d)
    tm_s = _row_tile(m, 512)
    tq = _row_tile(t, 512)
    outs = [[] for _ in range(5)]
    for l in range(DEPTH):
        w = lw[l]
        proj = lambda wi: norm_matmul(xf, w['norm1_g'], wi, tm_s, wi.shape[1]).reshape(b, t, -1)
        ya, sh, sa = rwkv_mixer(proj(w['w_in_a']), st_shift[l], st_rwkv[l], w, H_A, N_A, LORA_W, LORA_A)
        yb, sb = gla_mixer(proj(w['w_in_b']), st_gla[l], w, H_B, DK_B, DV_B, LORA_GK)
        yc, cb, sc = gdn_mixer(proj(w['w_in_c']), st_conv[l], st_gdn[l], w, H_C, DK_C, DV_C, CONV_W)
        x1 = out_proj(xf, ya.reshape(m, -1), yb.reshape(m, -1), yc.reshape(m, -1),
                      w['w_out_a'], w['w_out_b'], w['w_out_c'], tm_s)
        x2 = cross_attn(x1.reshape(b, t, d), w['normx_g'], w['wq_x'], w['wo_x'], mem_k[l], mem_v[l], H_X, tq)
        xf = mlp(x2.reshape(m, d), w['norm3_g'], final_g, w['w_up'], w['w_down'], tm_s, 1024,
                 final_norm=(l == DEPTH - 1))
        for lst, val in zip(outs, (sh, sa, sb, cb, sc)):
            lst.append(val)
    return (xf.reshape(b, t, d),) + tuple(jnp.stack(o) for o in outs)


def kernel(x_prompt, x_sample, state_rwkv_shift, state_rwkv, state_gla, state_gdn_conv, state_gdn, cache_mem_k, cache_mem_v, mem_prompt, norm1_g, w_in, mu_a, w0_a, w_w2, a0_a, w_a2, w_g2, k_k, k_a, r_k, lnx_g, lnx_b, w_gk2, b_gk, gla_norm_g, conv_w, a_log, dt_bias, gdn_norm_g, w_out, normx_g, normm_g, wq_x, wk_x, wv_x, wo_x, norm3_g, w_up, w_down, final_g):
    bf = lambda a: a.astype(BF16)
    w_in_a, w_in_b, w_in_c = split_cols(w_in, A_COLS, B_COLS, 256)
    lw = []
    for l in range(DEPTH):
        lw.append(dict(
            norm1_g=norm1_g[l],
            w_in_a=w_in_a[l], w_in_b=w_in_b[l], w_in_c=w_in_c[l],
            mu_a=mu_a[l], w0_a=w0_a[l], w_w2=w_w2[l], a0_a=a0_a[l], w_a2=w_a2[l], w_g2=w_g2[l],
            k_k=k_k[l], k_a=k_a[l], r_k=r_k[l], lnx_g=lnx_g[l], lnx_b=lnx_b[l],
            w_gk2=w_gk2[l], b_gk=b_gk[l], gla_norm_g=gla_norm_g[l],
            conv_w=conv_w[l], a_log=a_log[l], dt_bias=dt_bias[l], gdn_norm_g=gdn_norm_g[l],
            w_out_a=bf(w_out[l, :D_A]), w_out_b=bf(w_out[l, D_A:D_A + H_B * DV_B]),
            w_out_c=bf(w_out[l, D_A + H_B * DV_B:]),
            normx_g=normx_g[l], wq_x=bf(wq_x[l]), wo_x=bf(wo_x[l]),
            norm3_g=norm3_g[l], w_up=bf(w_up[l]), w_down=bf(w_down[l])))

    b, n_mem, d = mem_prompt.shape
    memf = mem_prompt.reshape(b * n_mem, d)
    hd_x = d // H_X
    tmm = _row_tile(b * n_mem, 1024)
    p_mem_k = jnp.stack([norm_matmul(memf, normm_g[l], bf(wk_x[l]), tmm, 512).reshape(b, n_mem, H_X, hd_x)
                         for l in range(DEPTH)])
    p_mem_v = jnp.stack([norm_matmul(memf, normm_g[l], bf(wv_x[l]), tmm, 512).reshape(b, n_mem, H_X, hd_x)
                         for l in range(DEPTH)])

    zeros = lambda *s: jnp.zeros((DEPTH, b) + s, F32)
    flat_mem = lambda a: bf(a.reshape(a.shape[0], a.shape[1], n_mem, d))
    y_prompt, p_shift, p_rwkv, p_gla, p_conv, p_gdn = _trunk(
        x_prompt, flat_mem(p_mem_k), flat_mem(p_mem_v),
        zeros(A_COLS), zeros(H_A, N_A, N_A), zeros(H_B, DK_B, DV_B), zeros(CONV_W - 1, C_QKV),
        zeros(H_C, DK_C, DV_C), lw, final_g)
    y_sample, s_shift, s_rwkv, s_gla, s_conv, s_gdn = _trunk(
        x_sample, flat_mem(cache_mem_k), flat_mem(cache_mem_v),
        state_rwkv_shift, state_rwkv, state_gla, state_gdn_conv, state_gdn, lw, final_g)
    return (y_prompt, y_sample, p_shift, p_rwkv, p_gla, p_conv, p_gdn, p_mem_k, p_mem_v,
            s_shift, s_rwkv, s_gla, s_conv, s_gdn)
```

```python
import functools

import jax
import jax.numpy as jnp
from jax import lax
from jax.experimental import pallas as pl
from jax.experimental.pallas import tpu as pltpu

F32 = jnp.float32
BF16 = jnp.bfloat16

EPS = 1e-6
RWKV_GN_EPS = 64e-5
GK_NORMALIZER = 16.0
CHUNK = 64
SUB = 16
LANE = 128
VMEM_LIMIT = 56 * 1024 * 1024


def _cparams(*sem):
    return pltpu.CompilerParams(dimension_semantics=sem, vmem_limit_bytes=VMEM_LIMIT)


MIN_CHAINS = 24


def _seq_block(b, heads):
    for d in range(1, b + 1):
        if b % d == 0 and d * heads >= MIN_CHAINS:
            return d
    return b


def _seq_groups(b, heads):
    g = _seq_block(b, heads)
    return g, (tuple(range(g)),)


def _dot(a, b):
    return jnp.dot(a.astype(BF16), b.astype(BF16), preferred_element_type=F32)


def _dot_nt(a, b):
    return lax.dot_general(a.astype(BF16), b.astype(BF16), (((1,), (1,)), ((), ())),
                           preferred_element_type=F32)


def _dot_tn(a, b):
    return lax.dot_general(a.astype(BF16), b.astype(BF16), (((0,), (0,)), ((), ())),
                           preferred_element_type=F32)


def _split(x, parts):
    out = []
    r = x
    for _ in range(parts):
        p = r.astype(BF16)
        out.append(p)
        r = r - p.astype(F32)
    return out


def _dot_exact_lhs(l_bf, x, parts=3):
    acc = None
    for p in _split(x, parts):
        d = jnp.dot(l_bf, p, preferred_element_type=F32)
        acc = d if acc is None else acc + d
    return acc


def _dot_exact_rhs(x, r_bf, parts=2):
    acc = None
    for p in _split(x, parts):
        d = jnp.dot(p, r_bf, preferred_element_type=F32)
        acc = d if acc is None else acc + d
    return acc


def _rms(x, g):
    ms = jnp.mean(x * x, axis=-1, keepdims=True)
    return x * lax.rsqrt(ms + EPS) * g


def _sigmoid(x):
    return 1.0 / (1.0 + jnp.exp(-x))


def _softplus(x):
    return jnp.maximum(x, 0.0) + jnp.log(1.0 + jnp.exp(-jnp.abs(x)))


def _tri(c, strict):
    r = lax.broadcasted_iota(jnp.int32, (c, c), 0)
    q = lax.broadcasted_iota(jnp.int32, (c, c), 1)
    return (r > q) if strict else (r >= q)


INV_BASE = 8


def _unit_lower_solve(ls, rhss):
    return _drain(_unit_lower_solve_stages(ls, rhss))


def _pair_diag(x):
    first = lax.broadcasted_iota(jnp.int32, x.shape, 1) < x.shape[1] // 2
    zero = jnp.zeros_like(x)
    return jnp.concatenate([jnp.where(first, x, zero), jnp.where(first, zero, x)], axis=0)


def _pair_dot(a, b):
    return _dot(a, _pair_diag(b.astype(BF16)))


def _unit_lower_solve_stages(ls, rhss, pair=False):
    c = ls[0].shape[0]
    mm = _pair_dot if pair else _dot
    r = lax.broadcasted_iota(jnp.int32, ls[0].shape, 0)
    q = lax.broadcasted_iota(jnp.int32, ls[0].shape, 1) % c
    eye = (r == q).astype(F32)
    base = min(INV_BASE, c)
    diag_blk = (r // base) == (q // base)
    ns = [jnp.where(diag_blk, -l, 0.0).astype(BF16) for l in ls]
    ts = [eye + n.astype(F32) for n in ns]
    p = 2
    while p < base:
        ns = [mm(n, n).astype(BF16) for n in ns]
        yield
        ts = [t + mm(t, n) for t, n in zip(ts, ns)]
        yield
        p *= 2
    s = base
    while s < c:
        sub_blk = ((r // s) % 2 == 1) & ((q // s) == (r // s) - 1)
        cs = [jnp.where(sub_blk, l, 0.0).astype(BF16) for l in ls]
        tb = [t.astype(BF16) for t in ts]
        cts = [mm(cm, t).astype(BF16) for cm, t in zip(cs, tb)]
        yield
        ts = [t - mm(b, ct) for t, b, ct in zip(ts, tb, cts)]
        yield
        s *= 2
    return [mm(t, rh) for t, rh in zip(ts, rhss)]


def _drain(gen):
    try:
        while True:
            next(gen)
    except StopIteration as stop:
        return stop.value


def _interleave(*gens):
    live = list(gens)
    while live:
        for g in list(live):
            try:
                next(g)
            except StopIteration:
                live.remove(g)


def _norm_matmul_kernel(x_ref, g_ref, w_ref, o_ref, h_ref):
    @pl.when(pl.program_id(1) == 0)
    def _():
        h_ref[...] = _rms(x_ref[...], g_ref[...]).astype(BF16)

    o_ref[...] = jnp.dot(h_ref[...], w_ref[...], preferred_element_type=F32).astype(o_ref.dtype)


def norm_matmul(x, g, w, tm, tn):
    m, k = x.shape
    n = w.shape[1]
    return pl.pallas_call(
        _norm_matmul_kernel,
        grid=(m // tm, n // tn),
        in_specs=[pl.BlockSpec((tm, k), lambda i, j: (i, 0)),
                  pl.BlockSpec((1, k), lambda i, j: (0, 0)),
                  pl.BlockSpec((k, tn), lambda i, j: (0, j))],
        out_specs=pl.BlockSpec((tm, tn), lambda i, j: (i, j)),
        out_shape=jax.ShapeDtypeStruct((m, n), F32),
        scratch_shapes=[pltpu.VMEM((tm, k), BF16)],
        compiler_params=_cparams("parallel", "arbitrary"),
    )(x, g.reshape(1, k), w)


def _split_cols_kernel(w_ref, a_ref, b_ref, c_ref, *, na, nb):
    n = w_ref.shape[2]
    nc = n - na - nb
    a_ref[0] = w_ref[0, :, 0:na].astype(BF16)
    b_ref[0, :, 0:nb] = w_ref[0, :, na:na + nb].astype(BF16)
    b_ref[0, :, nb:] = jnp.zeros((b_ref.shape[1], b_ref.shape[2] - nb), BF16)
    c_ref[0, :, 0:nc] = w_ref[0, :, na + nb:].astype(BF16)
    c_ref[0, :, nc:] = jnp.zeros((c_ref.shape[1], c_ref.shape[2] - nc), BF16)


def split_cols(w, na, nb, tk):
    depth, k, n = w.shape
    nc = n - na - nb
    widths = [na, nb + (-nb % LANE), nc + (-nc % LANE)]
    return pl.pallas_call(
        functools.partial(_split_cols_kernel, na=na, nb=nb),
        grid=(depth, k // tk),
        in_specs=[pl.BlockSpec((1, tk, n), lambda l, i: (l, i, 0))],
        out_specs=[pl.BlockSpec((1, tk, wd), lambda l, i: (l, i, 0)) for wd in widths],
        out_shape=[jax.ShapeDtypeStruct((depth, k, wd), BF16) for wd in widths],
        compiler_params=_cparams("parallel", "parallel"),
    )(w)


def _out_proj_kernel(x_ref, ya_ref, yb_ref, yc_ref, wa_ref, wb_ref, wc_ref, o_ref):
    acc = jnp.dot(ya_ref[...], wa_ref[...], preferred_element_type=F32)
    acc += jnp.dot(yb_ref[...], wb_ref[...], preferred_element_type=F32)
    acc += jnp.dot(yc_ref[...], wc_ref[...], preferred_element_type=F32)
    o_ref[...] = x_ref[...] + acc


def out_proj(x, ya, yb, yc, wa, wb, wc, tm):
    m, d = x.shape
    row = lambda a: pl.BlockSpec((tm, a.shape[1]), lambda i: (i, 0))
    full = lambda a: pl.BlockSpec(a.shape, lambda i: (0, 0))
    return pl.pallas_call(
        _out_proj_kernel,
        grid=(m // tm,),
        in_specs=[row(x), row(ya), row(yb), row(yc), full(wa), full(wb), full(wc)],
        out_specs=row(x),
        out_shape=jax.ShapeDtypeStruct((m, d), F32),
        compiler_params=_cparams("parallel"),
    )(x, ya, yb, yc, wa, wb, wc)


def _cross_attn_kernel(x_ref, g_ref, wq_ref, wo_ref, k_ref, v_ref, o_ref, *, heads):
    x = x_ref[0]
    d = x.shape[-1]
    hd = d // heads
    h = _rms(x, g_ref[...]).astype(BF16)
    q = jnp.dot(h, wq_ref[...], preferred_element_type=F32).astype(BF16)
    outs = []
    for i in range(heads):
        hs = slice(i * hd, (i + 1) * hd)
        s = _dot_nt(q[:, hs], k_ref[0, :, hs]) * (hd ** -0.5)
        s = s - jnp.max(s, axis=-1, keepdims=True)
        e = jnp.exp(s)
        pr = e / jnp.sum(e, axis=-1, keepdims=True)
        outs.append(jnp.dot(pr.astype(BF16), v_ref[0, :, hs], preferred_element_type=F32).astype(BF16))
    o = jnp.concatenate(outs, axis=-1)
    o_ref[0] = x + jnp.dot(o, wo_ref[...], preferred_element_type=F32)


def cross_attn(x, g, wq, wo, mk, mv, heads, tq):
    b, t, d = x.shape
    n_mem = mk.shape[1]
    full = lambda a: pl.BlockSpec(a.shape, lambda i, j: (0, 0))
    return pl.pallas_call(
        functools.partial(_cross_attn_kernel, heads=heads),
        grid=(b, t // tq),
        in_specs=[pl.BlockSpec((1, tq, d), lambda i, j: (i, j, 0)),
                  pl.BlockSpec((1, d), lambda i, j: (0, 0)),
                  full(wq), full(wo),
                  pl.BlockSpec((1, n_mem, d), lambda i, j: (i, 0, 0)),
                  pl.BlockSpec((1, n_mem, d), lambda i, j: (i, 0, 0))],
        out_specs=pl.BlockSpec((1, tq, d), lambda i, j: (i, j, 0)),
        out_shape=jax.ShapeDtypeStruct((b, t, d), F32),
        compiler_params=_cparams("parallel", "parallel"),
    )(x, g.reshape(1, d), wq, wo, mk, mv)


def _mlp_kernel(x_ref, g_ref, fg_ref, wu_ref, wd_ref, o_ref, h_ref, acc_ref, *, final_norm):
    f = pl.program_id(1)

    @pl.when(f == 0)
    def _():
        h_ref[...] = _rms(x_ref[...], g_ref[...]).astype(BF16)
        acc_ref[...] = jnp.zeros_like(acc_ref)

    u = jnp.maximum(jnp.dot(h_ref[...], wu_ref[...], preferred_element_type=F32), 0.0)
    acc_ref[...] += jnp.dot((u * u).astype(BF16), wd_ref[...], preferred_element_type=F32)

    @pl.when(f == pl.num_programs(1) - 1)
    def _():
        y = x_ref[...] + acc_ref[...]
        if final_norm:
            y = _rms(y, fg_ref[...])
        o_ref[...] = y


def mlp(x, g, fg, wu, wd, tm, tf, final_norm):
    m, d = x.shape
    dff = wu.shape[1]
    return pl.pallas_call(
        functools.partial(_mlp_kernel, final_norm=final_norm),
        grid=(m // tm, dff // tf),
        in_specs=[pl.BlockSpec((tm, d), lambda i, j: (i, 0)),
                  pl.BlockSpec((1, d), lambda i, j: (0, 0)),
                  pl.BlockSpec((1, d), lambda i, j: (0, 0)),
                  pl.BlockSpec((d, tf), lambda i, j: (0, j)),
                  pl.BlockSpec((tf, d), lambda i, j: (j, 0))],
        out_specs=pl.BlockSpec((tm, d), lambda i, j: (i, 0)),
        out_shape=jax.ShapeDtypeStruct((m, d), F32),
        scratch_shapes=[pltpu.VMEM((tm, d), BF16), pltpu.VMEM((tm, d), F32)],
        compiler_params=_cparams("parallel", "arbitrary"),
    )(x, g.reshape(1, d), fg.reshape(1, d), wu, wd)


def _rwkv_kernel(p_ref, shift0_ref, s0_ref, *rest, groups, **dims):
    n = pl.program_id(1)
    scr = rest[-2 * len(groups):]
    base = rest[:-2 * len(groups)]
    shift_ref, s_ref, carry_ref, state_ref = base[-4:]

    @pl.when(n == 0)
    def _():
        carry_ref[...] = shift0_ref[...]
        state_ref[...] = s0_ref[...]

    egls = {}
    front = [_rwkv_front(seqs, egls, p_ref, *base, scr[2 * gi], scr[2 * gi + 1], **dims)
             for gi, seqs in enumerate(groups)]
    chain = [_rwkv_chain(seqs, egls, p_ref, *base, scr[2 * gi], scr[2 * gi + 1], **dims)
             for gi, seqs in enumerate(groups)]
    _drain(front[0])
    for gi in range(1, len(groups)):
        _interleave(chain[gi - 1], front[gi])
    _drain(chain[-1])

    @pl.when(n == pl.num_programs(1) - 1)
    def _():
        shift_ref[...] = carry_ref[...]
        s_ref[...] = state_ref[...]


def _rwkv_front(seqs, egls, p_ref, mu_ref, w0_ref, ww2_ref, a0_ref, wa2_ref, wg2_ref,
                kk_ref, ka_ref, rk_ref, lng_ref, lnb_ref, seg_ref, segt_ref,
                y_ref, shift_ref, s_ref,
                carry_ref, state_ref, fe_ref, yh_ref, *, heads, hd, lora_w, lora_a):
    c = p_ref.shape[1]
    da = heads * hd
    g0 = seqs[0]

    seg = seg_ref[...]
    segt = segt_ref[...]
    seg_sum = lambda t: _head_sum(t, seg, segt)
    tril_bf = _tri(c, False).astype(BF16)

    for bi in seqs:
        pf = p_ref[bi]
        row = lax.broadcasted_iota(jnp.int32, pf.shape, 0)
        prev = jnp.where(row == 0, carry_ref[bi], pltpu.roll(pf, shift=1, axis=0))
        carry_ref[bi] = pf[c - 1:c, :]
        ps = pf + (prev - pf) * mu_ref[...]

        r = ps[:, 0:da]
        k = ps[:, da:2 * da]
        v = ps[:, 2 * da:3 * da]
        o = 3 * da
        xw = ps[:, o:o + lora_w]
        xa = ps[:, o + lora_w:o + lora_w + lora_a]
        xg = ps[:, o + lora_w + lora_a:]

        w_log = -_softplus(-(w0_ref[...] + _dot(jnp.tanh(xw), ww2_ref[...]))) - 0.5
        ld = -jnp.exp(w_log)
        a = _sigmoid(a0_ref[...] + _dot(xa, wa2_ref[...]))
        fe_ref[bi - g0, 8] = _dot(_sigmoid(xg), wg2_ref[...])
        yield

        kraw = k * kk_ref[...]
        kk = kraw * lax.rsqrt(seg_sum(kraw * kraw) + EPS)
        k2 = k * (1.0 + (a - 1.0) * ka_ref[...])
        b = kk * a
        yield

        gam = _dot_exact_lhs(tril_bf, ld)
        gl = gam[c - 1:c, :]
        eng = jnp.exp(-gam)
        el = jnp.exp(gl - gam)
        fe_ref[bi - g0, 0] = kk * jnp.exp(gam - ld)
        fe_ref[bi - g0, 1] = r * jnp.exp(gam)
        fe_ref[bi - g0, 2] = k2 * eng
        fe_ref[bi - g0, 3] = b * eng
        yield
        fe_ref[bi - g0, 4] = k2 * el
        fe_ref[bi - g0, 5] = b * el
        fe_ref[bi - g0, 6] = v
        fe_ref[bi - g0, 7] = seg_sum(r * k2 * rk_ref[...]) * v
        egls[bi] = jnp.exp(gl)
        yield


def _rwkv_chain(seqs, egls, p_ref, mu_ref, w0_ref, ww2_ref, a0_ref, wa2_ref, wg2_ref,
                kk_ref, ka_ref, rk_ref, lng_ref, lnb_ref, seg_ref, segt_ref,
                y_ref, shift_ref, s_ref,
                carry_ref, state_ref, fe_ref, yh_ref, *, heads, hd, lora_w, lora_a):
    c = p_ref.shape[1]
    g0 = seqs[0]
    seg = seg_ref[...]
    segt = segt_ref[...]
    seg_sum = lambda t: _head_sum(t, seg, segt)
    w = 2 * hd
    row = lax.broadcasted_iota(jnp.int32, (c, 2 * c), 0)
    col = lax.broadcasted_iota(jnp.int32, (c, 2 * c), 1) % c
    strict = row > col
    incl = row >= col
    first_head = lax.broadcasted_iota(jnp.int32, (hd, w), 1) < hd
    ch = [(bi, pi, slice(pi * w, (pi + 1) * w)) for bi in seqs for pi in range(heads // 2)]
    cr = range(len(ch))
    fe = lambda i, k: fe_ref[ch[i][0] - g0, k, :, ch[i][2]]
    lhs = [jnp.concatenate([fe(i, 0), fe(i, 1)], axis=0).astype(BF16) for i in cr]
    vd = [_pair_diag(fe(i, 6).astype(BF16)) for i in cr]
    s0 = [state_ref[bi, pi] for bi, pi, _ in ch]
    sk = [_dot_nt(lhs[i], _pair_diag(fe(i, 2).astype(BF16))) for i in cr]
    yield
    sb = [_dot_nt(lhs[i], _pair_diag(fe(i, 3).astype(BF16))) for i in cr]
    yield
    pa = [_dot_nt(lhs[i], _pair_diag(s0[i].astype(BF16))) for i in cr]
    yield
    rhs = [pa[i][:c] + _dot(jnp.where(strict, sk[i][:c], 0.0), vd[i]) for i in cr]
    yield
    u = yield from _unit_lower_solve_stages([jnp.where(strict, sb[i][:c], 0.0) for i in cr], rhs, pair=True)
    yield
    ya = [_dot(jnp.where(incl, sk[i][c:], 0.0), vd[i]) for i in cr]
    yield
    yb = [_pair_dot(jnp.where(incl, sb[i][c:], 0.0), u[i]) for i in cr]
    yield
    sa = [_dot_tn(fe(i, 6), fe(i, 4)) for i in cr]
    yield
    sc = [_dot_tn(u[i], fe(i, 5)) for i in cr]
    yield
    for i in cr:
        bi, pi, s = ch[i]
        yh_ref[bi - g0, :, s] = pa[i][c:] + ya[i] - yb[i]
        d = sa[i] - sc[i]
        state_ref[bi, pi] = s0[i] * egls[bi][:, s] + jnp.where(first_head, d[:hd], d[hd:])
    yield

    inv_n = 1.0 / hd
    for bi in seqs:
        y = yh_ref[bi - g0]
        mean = seg_sum(y) * inv_n
        dlt = y - mean
        var = seg_sum(dlt * dlt) * inv_n
        y = dlt * lax.rsqrt(var + RWKV_GN_EPS) * lng_ref[...] + lnb_ref[...]
        y_ref[bi] = ((y + fe_ref[bi - g0, 7]) * fe_ref[bi - g0, 8]).astype(y_ref.dtype)
        yield


def _head_sum(t, seg, segt):
    return _dot_exact_rhs(_dot_exact_rhs(t, seg, parts=1), segt)


def rwkv_mixer(p, shift0, s0, prm, heads, hd, lora_w, lora_a):
    b, t, cols = p.shape
    c = min(CHUNK, t)
    da = heads * hd
    seg = (jnp.arange(da)[:, None] // hd == jnp.arange(LANE)[None, :]).astype(BF16)
    segt = seg.T
    rowp = lambda a: a.reshape(1, -1)
    consts = [rowp(prm['mu_a']), rowp(prm['w0_a']), prm['w_w2'], rowp(prm['a0_a']), prm['w_a2'], prm['w_g2'],
              rowp(prm['k_k']), rowp(prm['k_a']), rowp(prm['r_k']), rowp(prm['lnx_g']), rowp(prm['lnx_b']),
              seg, segt]
    cspec = lambda a: pl.BlockSpec(a.shape, lambda i, j: (0, 0))
    bb, groups = _seq_groups(b, heads // 2)
    pair = lambda a: a.reshape(b, heads // 2, 2, hd, hd).swapaxes(2, 3).reshape(b, heads // 2, hd, 2 * hd)
    unpair = lambda a: a.reshape(b, heads // 2, hd, 2, hd).swapaxes(2, 3).reshape(b, heads, hd, hd)
    y, shift, s = pl.pallas_call(
        functools.partial(_rwkv_kernel, groups=groups, heads=heads, hd=hd, lora_w=lora_w, lora_a=lora_a),
        grid=(b // bb, t // c),
        in_specs=[pl.BlockSpec((bb, c, cols), lambda i, j: (i, j, 0)),
                  pl.BlockSpec((bb, 1, cols), lambda i, j: (i, 0, 0)),
                  pl.BlockSpec((bb, heads // 2, hd, 2 * hd), lambda i, j: (i, 0, 0, 0))] + [cspec(a) for a in consts],
        out_specs=[pl.BlockSpec((bb, c, da), lambda i, j: (i, j, 0)),
                   pl.BlockSpec((bb, 1, cols), lambda i, j: (i, 0, 0)),
                   pl.BlockSpec((bb, heads // 2, hd, 2 * hd), lambda i, j: (i, 0, 0, 0))],
        out_shape=[jax.ShapeDtypeStruct((b, t, da), BF16),
                   jax.ShapeDtypeStruct((b, 1, cols), F32),
                   jax.ShapeDtypeStruct((b, heads // 2, hd, 2 * hd), F32)],
        scratch_shapes=[pltpu.VMEM((bb, 1, cols), F32),
                        pltpu.VMEM((bb, heads // 2, hd, 2 * hd), F32)]
        + [pltpu.VMEM(shp, F32) for g in groups for shp in ((len(g), 9, c, da), (len(g), c, da))],
        compiler_params=_cparams("parallel", "arbitrary"),
    )(p, shift0.reshape(b, 1, cols), pair(s0), *consts)
    return y, shift.reshape(b, cols), unpair(s)


def _gla_kernel(p_ref, s0_ref, wgk_ref, bgk_ref, ng_ref, wseg_ref,
                y_ref, s_ref,
                state_ref, *, heads, dk, dv, lora):
    n = pl.program_id(1)
    bb, c, _ = p_ref.shape
    hk = heads * dk
    hv = heads * dv
    sub = min(SUB, c)
    nb = c // sub

    @pl.when(n == 0)
    def _():
        state_ref[...] = s0_ref[...]

    hr = range(heads)
    sr = range(bb)
    ksl = [slice(h * dk, (h + 1) * dk) for h in hr]
    vsl = [slice(h * dv, (h + 1) * dv) for h in hr]
    srow = lax.broadcasted_iota(jnp.int32, (sub, hk), 0)
    tril_bf = _tri(c, False).astype(BF16)
    wseg = wseg_ref[...]

    vs, gs, vbs, qgs, kds, egls, xss, qts, kts = [], [], [], [], [], [], [], [], []
    for si in sr:
        pf = p_ref[si]
        q = pf[:, 0:hk] * (dk ** -0.5)
        k = pf[:, hk:2 * hk]
        v = pf[:, 2 * hk:2 * hk + hv]
        gk = pf[:, 2 * hk + 2 * hv:2 * hk + 2 * hv + lora]
        z = _dot(gk, wgk_ref[...]) + bgk_ref[...]
        glog = (jnp.minimum(z, 0.0) - jnp.log(1.0 + jnp.exp(-jnp.abs(z)))) * (1.0 / GK_NORMALIZER)
        gc = _dot_exact_lhs(tril_bf, glog)
        gl = gc[c - 1:c, :]
        vs.append(v)
        gs.append(pf[:, 2 * hk + hv:2 * hk + 2 * hv])
        vbs.append(v.astype(BF16))
        qgs.append((q * jnp.exp(gc)).astype(BF16))
        kds.append((k * jnp.exp(gl - gc)).astype(BF16))
        egls.append(jnp.exp(gl))
        xs_blocks = []
        for bi in range(nb):
            r0 = bi * sub
            q_i, gc_i, k_i = q[r0:r0 + sub], gc[r0:r0 + sub], k[r0:r0 + sub]
            xs = []
            for jj in range(sub):
                e = jnp.exp(jnp.minimum(gc_i - gc_i[jj:jj + 1], 0.0))
                xs.append(jnp.where(srow >= jj, q_i * e * k_i[jj:jj + 1], 0.0).astype(BF16))
            xs_blocks.append(jnp.concatenate(xs, axis=0))
        xss.append(xs_blocks)
        qt, kt = [None], [None]
        for bi in range(1, nb):
            r0 = bi * sub
            ref_row = gc[r0:r0 + 1]
            qt.append((q[r0:r0 + sub] * jnp.exp(gc[r0:r0 + sub] - ref_row)).astype(BF16))
            kt.append((k[:r0] * jnp.exp(jnp.minimum(ref_row - gc[:r0], 0.0))).astype(BF16))
        qts.append(qt)
        kts.append(kt)

    s0 = [[state_ref[si, h] for h in hr] for si in sr]
    reds = [[jnp.dot(xss[si][bi], wseg, preferred_element_type=F32) for bi in range(nb)] for si in sr]
    o_in = [[_dot_nt(qgs[si][:, ksl[h]], s0[si][h]) for h in hr] for si in sr]
    sc = [[[_dot_nt(qts[si][bi][:, ksl[h]], kts[si][bi][:, ksl[h]]).astype(BF16) for h in hr]
           for bi in range(1, nb)] for si in sr]
    off = [[[_dot(sc[si][bi - 1][h], vbs[si][:bi * sub, vsl[h]]) for h in hr]
            for bi in range(1, nb)] for si in sr]
    s_up = [[_dot_tn(vbs[si][:, vsl[h]], kds[si][:, ksl[h]]) for h in hr] for si in sr]

    for si in sr:
        blocks = []
        for bi in range(nb):
            r0 = bi * sub
            v_i = vs[si][r0:r0 + sub]
            red = reds[si][bi]
            o_i = red[0:sub] * v_i[0:1]
            for jj in range(1, sub):
                o_i = o_i + red[jj * sub:(jj + 1) * sub] * v_i[jj:jj + 1]
            if bi > 0:
                o_i = o_i + jnp.concatenate(off[si][bi - 1], axis=-1)
            blocks.append(o_i)
        o = (jnp.concatenate(blocks, axis=0) if nb > 1 else blocks[0]) + jnp.concatenate(o_in[si], axis=-1)
        for h in hr:
            state_ref[si, h] = s0[si][h] * egls[si][:, ksl[h]] + s_up[si][h]
            o_h = _rms(o[:, vsl[h]], ng_ref[...])
            g_h = gs[si][:, vsl[h]]
            y_ref[si, :, vsl[h]] = (o_h * (g_h * _sigmoid(g_h))).astype(y_ref.dtype)

    @pl.when(n == pl.num_programs(1) - 1)
    def _():
        s_ref[...] = state_ref[...]


def gla_mixer(p, s0, prm, heads, dk, dv, lora):
    b, t, cols = p.shape
    c = min(CHUNK, t)
    hk, hv = heads * dk, heads * dv
    wseg = (jnp.arange(hk)[:, None] // dk == jnp.arange(hv)[None, :] // dv).astype(BF16)
    consts = [prm['w_gk2'], prm['b_gk'].reshape(1, hk), prm['gla_norm_g'].reshape(1, dv), wseg]
    cspec = lambda a: pl.BlockSpec(a.shape, lambda i, j: (0, 0))
    bb = _seq_block(b, heads)
    y, s = pl.pallas_call(
        functools.partial(_gla_kernel, heads=heads, dk=dk, dv=dv, lora=lora),
        grid=(b // bb, t // c),
        in_specs=[pl.BlockSpec((bb, c, cols), lambda i, j: (i, j, 0)),
                  pl.BlockSpec((bb, heads, dv, dk), lambda i, j: (i, 0, 0, 0))] + [cspec(a) for a in consts],
        out_specs=[pl.BlockSpec((bb, c, hv), lambda i, j: (i, j, 0)),
                   pl.BlockSpec((bb, heads, dv, dk), lambda i, j: (i, 0, 0, 0))],
        out_shape=[jax.ShapeDtypeStruct((b, t, hv), BF16),
                   jax.ShapeDtypeStruct((b, heads, dv, dk), F32)],
        scratch_shapes=[pltpu.VMEM((bb, heads, dv, dk), F32)],
        compiler_params=_cparams("parallel", "arbitrary"),
    )(p, jnp.swapaxes(s0, -1, -2), *consts)
    return y, jnp.swapaxes(s, -1, -2)


def _gdn_kernel(p_ref, conv0_ref, s0_ref, cw_ref, alog_ref, dtb_ref, ng_ref,
                y_ref, conv_ref, s_ref,
                xp_ref, state_ref, *, heads, dk, dv, conv_w):
    n = pl.program_id(1)
    bb, c, _ = p_ref.shape
    cq = 2 * heads * dk + heads * dv
    pad = 8
    base = pad - (conv_w - 1)

    @pl.when(n == 0)
    def _():
        xp_ref[:, 0:pad, :] = conv0_ref[...]
        state_ref[...] = s0_ref[...]

    tril = _tri(c, False)
    strict = _tri(c, True)
    tril_bf = tril.astype(BF16)
    triu_bf = (lax.broadcasted_iota(jnp.int32, (c, c), 0) <= lax.broadcasted_iota(jnp.int32, (c, c), 1)).astype(BF16)

    ch = [(si, h) for si in range(bb) for h in range(heads)]
    cr = range(len(ch))
    qs, qgs, ks, kds, kbs, rhss, decs, egl_last = [], [], [], [], [], [], [], []
    for si in range(bb):
        xp_ref[si, pad:pad + c, :] = p_ref[si, :, 0:cq]
        conv = xp_ref[si, base:base + c, :] * cw_ref[0:1, :]
        for w in range(1, conv_w):
            conv = conv + xp_ref[si, base + w:base + w + c, :] * cw_ref[w:w + 1, :]
        xp_ref[si, 0:pad, :] = xp_ref[si, c:c + pad, :]
        qkv = conv * _sigmoid(conv)

        ab = p_ref[si, :, cq + heads * dv:]
        beta = _sigmoid(ab)
        g = -jnp.exp(alog_ref[...]) * _softplus(ab + dtb_ref[...])
        gc = _dot_exact_lhs(tril_bf, g)
        gct = None
        for part in _split(g, 3):
            d = lax.dot_general(part, triu_bf, (((0,), (0,)), ((), ())), preferred_element_type=F32)
            gct = d if gct is None else gct + d
        gl = gc[c - 1:c, :]
        eg = jnp.exp(gc)
        egl = jnp.exp(gl - gc)
        egl_last.append(jnp.exp(gl))
        for h in range(heads):
            q = qkv[:, h * dk:(h + 1) * dk]
            k = qkv[:, heads * dk + h * dk:heads * dk + (h + 1) * dk]
            v = qkv[:, 2 * heads * dk + h * dv:2 * heads * dk + (h + 1) * dv]
            q = q * lax.rsqrt(jnp.sum(q * q, axis=-1, keepdims=True) + EPS) * (dk ** -0.5)
            k = k * lax.rsqrt(jnp.sum(k * k, axis=-1, keepdims=True) + EPS)
            bcol = beta[:, heads + h:heads + h + 1]
            diff = gc[:, h:h + 1] - gct[h:h + 1, :]
            decs.append(jnp.where(tril, jnp.exp(jnp.where(tril, diff, 0.0)), 0.0))
            kb = k * bcol
            qs.append(q.astype(BF16))
            qgs.append((q * eg[:, h:h + 1]).astype(BF16))
            ks.append(k.astype(BF16))
            kds.append((k * egl[:, h:h + 1]).astype(BF16))
            kbs.append(kb.astype(BF16))
            rhss.append(jnp.concatenate([v * bcol, kb * eg[:, h:h + 1]], axis=-1))
    lmat = [jnp.where(strict, _dot_nt(kbs[i], ks[i]) * decs[i], 0.0) for i in cr]
    att = [(_dot_nt(qs[i], ks[i]) * decs[i]).astype(BF16) for i in cr]
    uw = _unit_lower_solve(lmat, rhss)
    s0 = [state_ref[si, h] for si, h in ch]
    s0b = [s.astype(BF16) for s in s0]
    v_new = [uw[i][:, :dv] - _dot(uw[i][:, dv:], s0b[i]) for i in cr]
    o_in = [_dot(qgs[i], s0b[i]) for i in cr]
    o_at = [_dot(att[i], v_new[i]) for i in cr]
    s_up = [_dot_tn(kds[i], v_new[i]) for i in cr]
    for i in cr:
        si, h = ch[i]
        state_ref[si, h] = s0[i] * egl_last[si][:, h:h + 1] + s_up[i]
        o = _rms(o_in[i] + o_at[i], ng_ref[...])
        z_h = p_ref[si, :, cq + h * dv:cq + (h + 1) * dv]
        y_ref[si, :, h * dv:(h + 1) * dv] = (o * (z_h * _sigmoid(z_h))).astype(y_ref.dtype)

    @pl.when(n == pl.num_programs(1) - 1)
    def _():
        conv_ref[...] = xp_ref[:, 0:pad, :]
        s_ref[...] = state_ref[...]


def gdn_mixer(p, conv0, s0, prm, heads, dk, dv, conv_w):
    b, t, cols = p.shape
    c = min(CHUNK, t)
    cq = 2 * heads * dk + heads * dv
    pad = 8
    lanep = lambda a: jnp.pad(a.reshape(1, -1), ((0, 0), (0, LANE - a.size)))
    conv0p = jnp.pad(conv0, ((0, 0), (pad - (conv_w - 1), 0), (0, 0)))
    cw = jnp.pad(prm['conv_w'], ((0, pad - conv_w), (0, 0)))
    consts = [cw, lanep(prm['a_log']), lanep(prm['dt_bias']), prm['gdn_norm_g'].reshape(1, dv)]
    cspec = lambda a: pl.BlockSpec(a.shape, lambda i, j: (0, 0))
    bb = _seq_block(b, heads)
    y, conv, s = pl.pallas_call(
        functools.partial(_gdn_kernel, heads=heads, dk=dk, dv=dv, conv_w=conv_w),
        grid=(b // bb, t // c),
        in_specs=[pl.BlockSpec((bb, c, cols), lambda i, j: (i, j, 0)),
                  pl.BlockSpec((bb, pad, cq), lambda i, j: (i, 0, 0)),
                  pl.BlockSpec((bb, heads, dk, dv), lambda i, j: (i, 0, 0, 0))] + [cspec(a) for a in consts],
        out_specs=[pl.BlockSpec((bb, c, heads * dv), lambda i, j: (i, j, 0)),
                   pl.BlockSpec((bb, pad, cq), lambda i, j: (i, 0, 0)),
                   pl.BlockSpec((bb, heads, dk, dv), lambda i, j: (i, 0, 0, 0))],
        out_shape=[jax.ShapeDtypeStruct((b, t, heads * dv), BF16),
                   jax.ShapeDtypeStruct((b, pad, cq), F32),
                   jax.ShapeDtypeStruct((b, heads, dk, dv), F32)],
        scratch_shapes=[pltpu.VMEM((bb, c + pad, cq), F32),
                        pltpu.VMEM((bb, heads, dk, dv), F32)],
        compiler_params=_cparams("parallel", "arbitrary"),
    )(p, conv0p, s0, *consts)
    return y, conv[:, pad - (conv_w - 1):, :], s


D_MODEL = 2048
DEPTH = 2
H_A, N_A = 12, 64
D_A = H_A * N_A
LORA_W, LORA_A, LORA_G = 64, 64, 128
H_B, DK_B, DV_B, LORA_GK = 4, 64, 128, 16
H_C, DK_C, DV_C, CONV_W = 6, 128, 128, 4
A_COLS = 3 * D_A + LORA_W + LORA_A + LORA_G
B_COLS = 2 * H_B * DK_B + 2 * H_B * DV_B + LORA_GK
C_QKV = 2 * H_C * DK_C + H_C * DV_C
C_COLS = C_QKV + H_C * DV_C + 2 * H_C
H_X = 4


def _row_tile(m, cap):
    t = min(m, cap)
    while m % t:
        t //= 2
    return t


def _trunk(x, mem_k, mem_v, st_shift, st_rwkv, st_gla, st_conv, st_gdn, lw, final_g):
    b, t, d = x.shape
    m = b * t
    xf = x.reshape(m, d)
    tm_s = _row_tile(m, 512)
    tq = _row_tile(t, 512)
    outs = [[] for _ in range(5)]
    for l in range(DEPTH):
        w = lw[l]
        proj = lambda wi: norm_matmul(xf, w['norm1_g'], wi, tm_s, wi.shape[1]).reshape(b, t, -1)
        ya, sh, sa = rwkv_mixer(proj(w['w_in_a']), st_shift[l], st_rwkv[l], w, H_A, N_A, LORA_W, LORA_A)
        yb, sb = gla_mixer(proj(w['w_in_b']), st_gla[l], w, H_B, DK_B, DV_B, LORA_GK)
        yc, cb, sc = gdn_mixer(proj(w['w_in_c']), st_conv[l], st_gdn[l], w, H_C, DK_C, DV_C, CONV_W)
        x1 = out_proj(xf, ya.reshape(m, -1), yb.reshape(m, -1), yc.reshape(m, -1),
                      w['w_out_a'], w['w_out_b'], w['w_out_c'], tm_s)
        x2 = cross_attn(x1.reshape(b, t, d), w['normx_g'], w['wq_x'], w['wo_x'], mem_k[l], mem_v[l], H_X, tq)
        xf = mlp(x2.reshape(m, d), w['norm3_g'], final_g, w['w_up'], w['w_down'], tm_s, 1024,
                 final_norm=(l == DEPTH - 1))
        for lst, val in zip(outs, (sh, sa, sb, cb, sc)):
            lst.append(val)
    return (xf.reshape(b, t, d),) + tuple(jnp.stack(o) for o in outs)


def kernel(x_prompt, x_sample, state_rwkv_shift, state_rwkv, state_gla, state_gdn_conv, state_gdn, cache_mem_k, cache_mem_v, mem_prompt, norm1_g, w_in, mu_a, w0_a, w_w2, a0_a, w_a2, w_g2, k_k, k_a, r_k, lnx_g, lnx_b, w_gk2, b_gk, gla_norm_g, conv_w, a_log, dt_bias, gdn_norm_g, w_out, normx_g, normm_g, wq_x, wk_x, wv_x, wo_x, norm3_g, w_up, w_down, final_g):
    bf = lambda a: a.astype(BF16)
    w_in_a, w_in_b, w_in_c = split_cols(w_in, A_COLS, B_COLS, 256)
    lw = []
    for l in range(DEPTH):
        lw.append(dict(
            norm1_g=norm1_g[l],
            w_in_a=w_in_a[l], w_in_b=w_in_b[l], w_in_c=w_in_c[l],
            mu_a=mu_a[l], w0_a=w0_a[l], w_w2=w_w2[l], a0_a=a0_a[l], w_a2=w_a2[l], w_g2=w_g2[l],
            k_k=k_k[l], k_a=k_a[l], r_k=r_k[l], lnx_g=lnx_g[l], lnx_b=lnx_b[l],
            w_gk2=w_gk2[l], b_gk=b_gk[l], gla_norm_g=gla_norm_g[l],
            conv_w=conv_w[l], a_log=a_log[l], dt_bias=dt_bias[l], gdn_norm_g=gdn_norm_g[l],
            w_out_a=bf(w_out[l, :D_A]), w_out_b=bf(w_out[l, D_A:D_A + H_B * DV_B]),
            w_out_c=bf(w_out[l, D_A + H_B * DV_B:]),
            normx_g=normx_g[l], wq_x=bf(wq_x[l]), wo_x=bf(wo_x[l]),
            norm3_g=norm3_g[l], w_up=bf(w_up[l]), w_down=bf(w_down[l])))

    b, n_mem, d = mem_prompt.shape
    memf = mem_prompt.reshape(b * n_mem, d)
    hd_x = d // H_X
    tmm = _row_tile(b * n_mem, 1024)
    p_mem_k = jnp.stack([norm_matmul(memf, normm_g[l], bf(wk_x[l]), tmm, 512).reshape(b, n_mem, H_X, hd_x)
                         for l in range(DEPTH)])
    p_mem_v = jnp.stack([norm_matmul(memf, normm_g[l], bf(wv_x[l]), tmm, 512).reshape(b, n_mem, H_X, hd_x)
                         for l in range(DEPTH)])

    zeros = lambda *s: jnp.zeros((DEPTH, b) + s, F32)
    flat_mem = lambda a: bf(a.reshape(a.shape[0], a.shape[1], n_mem, d))
    y_prompt, p_shift, p_rwkv, p_gla, p_conv, p_gdn = _trunk(
        x_prompt, flat_mem(p_mem_k), flat_mem(p_mem_v),
        zeros(A_COLS), zeros(H_A, N_A, N_A), zeros(H_B, DK_B, DV_B), zeros(CONV_W - 1, C_QKV),
        zeros(H_C, DK_C, DV_C), lw, final_g)
    y_sample, s_shift, s_rwkv, s_gla, s_conv, s_gdn = _trunk(
        x_sample, flat_mem(cache_mem_k), flat_mem(cache_mem_v),
        state_rwkv_shift, state_rwkv, state_gla, state_gdn_conv, state_gdn, lw, final_g)
    return (y_prompt, y_sample, p_shift, p_rwkv, p_gla, p_conv, p_gdn, p_mem_k, p_mem_v,
            s_shift, s_rwkv, s_gla, s_conv, s_gdn)
```

```python
import functools

import jax
import jax.numpy as jnp
from jax import lax
from jax.experimental import pallas as pl
from jax.experimental.pallas import tpu as pltpu

F32 = jnp.float32
BF16 = jnp.bfloat16

EPS = 1e-6
RWKV_GN_EPS = 64e-5
GK_NORMALIZER = 16.0
CHUNK = 64
SUB = 16
LANE = 128
VMEM_LIMIT = 56 * 1024 * 1024


def _cparams(*sem):
    return pltpu.CompilerParams(dimension_semantics=sem, vmem_limit_bytes=VMEM_LIMIT)


MIN_CHAINS = 24


def _seq_block(b, heads):
    for d in range(1, b + 1):
        if b % d == 0 and d * heads >= MIN_CHAINS:
            return d
    return b


def _seq_groups(b, heads):
    g = _seq_block(b, heads)
    return g, (tuple(range(g)),)


def _dot(a, b):
    return jnp.dot(a.astype(BF16), b.astype(BF16), preferred_element_type=F32)


def _dot_nt(a, b):
    return lax.dot_general(a.astype(BF16), b.astype(BF16), (((1,), (1,)), ((), ())),
                           preferred_element_type=F32)


def _dot_tn(a, b):
    return lax.dot_general(a.astype(BF16), b.astype(BF16), (((0,), (0,)), ((), ())),
                           preferred_element_type=F32)


def _split(x, parts):
    out = []
    r = x
    for _ in range(parts):
        p = r.astype(BF16)
        out.append(p)
        r = r - p.astype(F32)
    return out


def _dot_exact_lhs(l_bf, x, parts=3):
    acc = None
    for p in _split(x, parts):
        d = jnp.dot(l_bf, p, preferred_element_type=F32)
        acc = d if acc is None else acc + d
    return acc


def _dot_exact_rhs(x, r_bf, parts=2):
    acc = None
    for p in _split(x, parts):
        d = jnp.dot(p, r_bf, preferred_element_type=F32)
        acc = d if acc is None else acc + d
    return acc


def _rms(x, g):
    ms = jnp.mean(x * x, axis=-1, keepdims=True)
    return x * lax.rsqrt(ms + EPS) * g


def _sigmoid(x):
    return 1.0 / (1.0 + jnp.exp(-x))


def _softplus(x):
    return jnp.maximum(x, 0.0) + jnp.log(1.0 + jnp.exp(-jnp.abs(x)))


def _tri(c, strict):
    r = lax.broadcasted_iota(jnp.int32, (c, c), 0)
    q = lax.broadcasted_iota(jnp.int32, (c, c), 1)
    return (r > q) if strict else (r >= q)


INV_BASE = 8


def _unit_lower_solve(ls, rhss):
    return _drain(_unit_lower_solve_stages(ls, rhss))


def _pair_diag(x):
    first = lax.broadcasted_iota(jnp.int32, x.shape, 1) < x.shape[1] // 2
    zero = jnp.zeros_like(x)
    return jnp.concatenate([jnp.where(first, x, zero), jnp.where(first, zero, x)], axis=0)


def _pair_dot(a, b):
    return _dot(a, _pair_diag(b.astype(BF16)))


def _unit_lower_solve_stages(ls, rhss, pair=False):
    c = ls[0].shape[0]
    mm = _pair_dot if pair else _dot
    r = lax.broadcasted_iota(jnp.int32, ls[0].shape, 0)
    q = lax.broadcasted_iota(jnp.int32, ls[0].shape, 1) % c
    eye = (r == q).astype(F32)
    base = min(INV_BASE, c)
    diag_blk = (r // base) == (q // base)
    ns = [jnp.where(diag_blk, -l, 0.0).astype(BF16) for l in ls]
    ts = [eye + n.astype(F32) for n in ns]
    p = 2
    while p < base:
        ns = [mm(n, n).astype(BF16) for n in ns]
        yield
        ts = [t + mm(t, n) for t, n in zip(ts, ns)]
        yield
        p *= 2
    s = base
    while s < c:
        sub_blk = ((r // s) % 2 == 1) & ((q // s) == (r // s) - 1)
        cs = [jnp.where(sub_blk, l, 0.0).astype(BF16) for l in ls]
        tb = [t.astype(BF16) for t in ts]
        cts = [mm(cm, t).astype(BF16) for cm, t in zip(cs, tb)]
        yield
        ts = [t - mm(b, ct) for t, b, ct in zip(ts, tb, cts)]
        yield
        s *= 2
    return [mm(t, rh) for t, rh in zip(ts, rhss)]


def _drain(gen):
    try:
        while True:
            next(gen)
    except StopIteration as stop:
        return stop.value


def _interleave(*gens):
    live = list(gens)
    while live:
        for g in list(live):
            try:
                next(g)
            except StopIteration:
                live.remove(g)


def _norm_matmul_kernel(x_ref, g_ref, w_ref, o_ref, h_ref):
    @pl.when(pl.program_id(1) == 0)
    def _():
        h_ref[...] = _rms(x_ref[...], g_ref[...]).astype(BF16)

    o_ref[...] = jnp.dot(h_ref[...], w_ref[...], preferred_element_type=F32).astype(o_ref.dtype)


def norm_matmul(x, g, w, tm, tn):
    m, k = x.shape
    n = w.shape[1]
    return pl.pallas_call(
        _norm_matmul_kernel,
        grid=(m // tm, n // tn),
        in_specs=[pl.BlockSpec((tm, k), lambda i, j: (i, 0)),
                  pl.BlockSpec((1, k), lambda i, j: (0, 0)),
                  pl.BlockSpec((k, tn), lambda i, j: (0, j))],
        out_specs=pl.BlockSpec((tm, tn), lambda i, j: (i, j)),
        out_shape=jax.ShapeDtypeStruct((m, n), F32),
        scratch_shapes=[pltpu.VMEM((tm, k), BF16)],
        compiler_params=_cparams("parallel", "arbitrary"),
    )(x, g.reshape(1, k), w)


def _split_cols_kernel(w_ref, a_ref, b_ref, c_ref, *, na, nb):
    n = w_ref.shape[2]
    nc = n - na - nb
    a_ref[0] = w_ref[0, :, 0:na].astype(BF16)
    b_ref[0, :, 0:nb] = w_ref[0, :, na:na + nb].astype(BF16)
    b_ref[0, :, nb:] = jnp.zeros((b_ref.shape[1], b_ref.shape[2] - nb), BF16)
    c_ref[0, :, 0:nc] = w_ref[0, :, na + nb:].astype(BF16)
    c_ref[0, :, nc:] = jnp.zeros((c_ref.shape[1], c_ref.shape[2] - nc), BF16)


def split_cols(w, na, nb, tk):
    depth, k, n = w.shape
    nc = n - na - nb
    widths = [na, nb + (-nb % LANE), nc + (-nc % LANE)]
    return pl.pallas_call(
        functools.partial(_split_cols_kernel, na=na, nb=nb),
        grid=(depth, k // tk),
        in_specs=[pl.BlockSpec((1, tk, n), lambda l, i: (l, i, 0))],
        out_specs=[pl.BlockSpec((1, tk, wd), lambda l, i: (l, i, 0)) for wd in widths],
        out_shape=[jax.ShapeDtypeStruct((depth, k, wd), BF16) for wd in widths],
        compiler_params=_cparams("parallel", "parallel"),
    )(w)


def _out_proj_kernel(x_ref, ya_ref, yb_ref, yc_ref, wa_ref, wb_ref, wc_ref, o_ref):
    acc = jnp.dot(ya_ref[...], wa_ref[...], preferred_element_type=F32)
    acc += jnp.dot(yb_ref[...], wb_ref[...], preferred_element_type=F32)
    acc += jnp.dot(yc_ref[...], wc_ref[...], preferred_element_type=F32)
    o_ref[...] = x_ref[...] + acc


def out_proj(x, ya, yb, yc, wa, wb, wc, tm):
    m, d = x.shape
    row = lambda a: pl.BlockSpec((tm, a.shape[1]), lambda i: (i, 0))
    full = lambda a: pl.BlockSpec(a.shape, lambda i: (0, 0))
    return pl.pallas_call(
        _out_proj_kernel,
        grid=(m // tm,),
        in_specs=[row(x), row(ya), row(yb), row(yc), full(wa), full(wb), full(wc)],
        out_specs=row(x),
        out_shape=jax.ShapeDtypeStruct((m, d), F32),
        compiler_params=_cparams("parallel"),
    )(x, ya, yb, yc, wa, wb, wc)


def _cross_attn_kernel(x_ref, g_ref, wq_ref, wo_ref, k_ref, v_ref, o_ref, *, heads):
    x = x_ref[0]
    d = x.shape[-1]
    hd = d // heads
    h = _rms(x, g_ref[...]).astype(BF16)
    q = jnp.dot(h, wq_ref[...], preferred_element_type=F32).astype(BF16)
    outs = []
    for i in range(heads):
        hs = slice(i * hd, (i + 1) * hd)
        s = _dot_nt(q[:, hs], k_ref[0, :, hs]) * (hd ** -0.5)
        s = s - jnp.max(s, axis=-1, keepdims=True)
        e = jnp.exp(s)
        pr = e / jnp.sum(e, axis=-1, keepdims=True)
        outs.append(jnp.dot(pr.astype(BF16), v_ref[0, :, hs], preferred_element_type=F32).astype(BF16))
    o = jnp.concatenate(outs, axis=-1)
    o_ref[0] = x + jnp.dot(o, wo_ref[...], preferred_element_type=F32)


def cross_attn(x, g, wq, wo, mk, mv, heads, tq):
    b, t, d = x.shape
    n_mem = mk.shape[1]
    full = lambda a: pl.BlockSpec(a.shape, lambda i, j: (0, 0))
    return pl.pallas_call(
        functools.partial(_cross_attn_kernel, heads=heads),
        grid=(b, t // tq),
        in_specs=[pl.BlockSpec((1, tq, d), lambda i, j: (i, j, 0)),
                  pl.BlockSpec((1, d), lambda i, j: (0, 0)),
                  full(wq), full(wo),
                  pl.BlockSpec((1, n_mem, d), lambda i, j: (i, 0, 0)),
                  pl.BlockSpec((1, n_mem, d), lambda i, j: (i, 0, 0))],
        out_specs=pl.BlockSpec((1, tq, d), lambda i, j: (i, j, 0)),
        out_shape=jax.ShapeDtypeStruct((b, t, d), F32),
        compiler_params=_cparams("parallel", "parallel"),
    )(x, g.reshape(1, d), wq, wo, mk, mv)


def _mlp_kernel(x_ref, g_ref, fg_ref, wu_ref, wd_ref, o_ref, h_ref, acc_ref, *, final_norm):
    f = pl.program_id(1)

    @pl.when(f == 0)
    def _():
        h_ref[...] = _rms(x_ref[...], g_ref[...]).astype(BF16)
        acc_ref[...] = jnp.zeros_like(acc_ref)

    u = jnp.maximum(jnp.dot(h_ref[...], wu_ref[...], preferred_element_type=F32), 0.0)
    acc_ref[...] += jnp.dot((u * u).astype(BF16), wd_ref[...], preferred_element_type=F32)

    @pl.when(f == pl.num_programs(1) - 1)
    def _():
        y = x_ref[...] + acc_ref[...]
        if final_norm:
            y = _rms(y, fg_ref[...])
        o_ref[...] = y


def mlp(x, g, fg, wu, wd, tm, tf, final_norm):
    m, d = x.shape
    dff = wu.shape[1]
    return pl.pallas_call(
        functools.partial(_mlp_kernel, final_norm=final_norm),
        grid=(m // tm, dff // tf),
        in_specs=[pl.BlockSpec((tm, d), lambda i, j: (i, 0)),
                  pl.BlockSpec((1, d), lambda i, j: (0, 0)),
                  pl.BlockSpec((1, d), lambda i, j: (0, 0)),
                  pl.BlockSpec((d, tf), lambda i, j: (0, j)),
                  pl.BlockSpec((tf, d), lambda i, j: (j, 0))],
        out_specs=pl.BlockSpec((tm, d), lambda i, j: (i, 0)),
        out_shape=jax.ShapeDtypeStruct((m, d), F32),
        scratch_shapes=[pltpu.VMEM((tm, d), BF16), pltpu.VMEM((tm, d), F32)],
        compiler_params=_cparams("parallel", "arbitrary"),
    )(x, g.reshape(1, d), fg.reshape(1, d), wu, wd)


def _rwkv_kernel(p_ref, shift0_ref, s0_ref, *rest, groups, **dims):
    n = pl.program_id(1)
    scr = rest[-2 * len(groups):]
    base = rest[:-2 * len(groups)]
    shift_ref, s_ref, carry_ref, state_ref = base[-4:]

    @pl.when(n == 0)
    def _():
        carry_ref[...] = shift0_ref[...]
        state_ref[...] = s0_ref[...]

    egls = {}
    front = [_rwkv_front(seqs, egls, p_ref, *base, scr[2 * gi], scr[2 * gi + 1], **dims)
             for gi, seqs in enumerate(groups)]
    chain = [_rwkv_chain(seqs, egls, p_ref, *base, scr[2 * gi], scr[2 * gi + 1], **dims)
             for gi, seqs in enumerate(groups)]
    _drain(front[0])
    for gi in range(1, len(groups)):
        _interleave(chain[gi - 1], front[gi])
    _drain(chain[-1])

    @pl.when(n == pl.num_programs(1) - 1)
    def _():
        shift_ref[...] = carry_ref[...]
        s_ref[...] = state_ref[...]


def _rwkv_front(seqs, egls, p_ref, mu_ref, w0_ref, ww2_ref, a0_ref, wa2_ref, wg2_ref,
                kk_ref, ka_ref, rk_ref, lng_ref, lnb_ref, seg_ref, segt_ref,
                y_ref, shift_ref, s_ref,
                carry_ref, state_ref, fe_ref, yh_ref, *, heads, hd, lora_w, lora_a):
    c = p_ref.shape[1]
    da = heads * hd
    g0 = seqs[0]

    seg = seg_ref[...]
    segt = segt_ref[...]
    seg_sum = lambda t: _head_sum(t, seg, segt)
    tril_bf = _tri(c, False).astype(BF16)

    for bi in seqs:
        pf = p_ref[bi]
        row = lax.broadcasted_iota(jnp.int32, pf.shape, 0)
        prev = jnp.where(row == 0, carry_ref[bi], pltpu.roll(pf, shift=1, axis=0))
        carry_ref[bi] = pf[c - 1:c, :]
        ps = pf + (prev - pf) * mu_ref[...]

        r = ps[:, 0:da]
        k = ps[:, da:2 * da]
        v = ps[:, 2 * da:3 * da]
        o = 3 * da
        xw = ps[:, o:o + lora_w]
        xa = ps[:, o + lora_w:o + lora_w + lora_a]
        xg = ps[:, o + lora_w + lora_a:]

        w_log = -_softplus(-(w0_ref[...] + _dot(jnp.tanh(xw), ww2_ref[...]))) - 0.5
        ld = -jnp.exp(w_log)
        a = _sigmoid(a0_ref[...] + _dot(xa, wa2_ref[...]))
        fe_ref[bi - g0, 8] = _dot(_sigmoid(xg), wg2_ref[...])
        yield

        kraw = k * kk_ref[...]
        kk = kraw * lax.rsqrt(seg_sum(kraw * kraw) + EPS)
        k2 = k * (1.0 + (a - 1.0) * ka_ref[...])
        b = kk * a
        yield

        gam = _dot_exact_lhs(tril_bf, ld)
        gl = gam[c - 1:c, :]
        eng = jnp.exp(-gam)
        el = jnp.exp(gl - gam)
        fe_ref[bi - g0, 0] = kk * jnp.exp(gam - ld)
        fe_ref[bi - g0, 1] = r * jnp.exp(gam)
        fe_ref[bi - g0, 2] = k2 * eng
        fe_ref[bi - g0, 3] = b * eng
        yield
        fe_ref[bi - g0, 4] = k2 * el
        fe_ref[bi - g0, 5] = b * el
        fe_ref[bi - g0, 6] = v
        fe_ref[bi - g0, 7] = seg_sum(r * k2 * rk_ref[...]) * v
        egls[bi] = jnp.exp(gl)
        yield


def _rwkv_chain(seqs, egls, p_ref, mu_ref, w0_ref, ww2_ref, a0_ref, wa2_ref, wg2_ref,
                kk_ref, ka_ref, rk_ref, lng_ref, lnb_ref, seg_ref, segt_ref,
                y_ref, shift_ref, s_ref,
                carry_ref, state_ref, fe_ref, yh_ref, *, heads, hd, lora_w, lora_a):
    c = p_ref.shape[1]
    g0 = seqs[0]
    seg = seg_ref[...]
    segt = segt_ref[...]
    seg_sum = lambda t: _head_sum(t, seg, segt)
    w = 2 * hd
    row = lax.broadcasted_iota(jnp.int32, (c, 2 * c), 0)
    col = lax.broadcasted_iota(jnp.int32, (c, 2 * c), 1) % c
    strict = row > col
    incl = row >= col
    first_head = lax.broadcasted_iota(jnp.int32, (hd, w), 1) < hd
    ch = [(bi, pi, slice(pi * w, (pi + 1) * w)) for bi in seqs for pi in range(heads // 2)]
    cr = range(len(ch))
    fe = lambda i, k: fe_ref[ch[i][0] - g0, k, :, ch[i][2]]
    lhs = [jnp.concatenate([fe(i, 0), fe(i, 1)], axis=0).astype(BF16) for i in cr]
    vd = [_pair_diag(fe(i, 6).astype(BF16)) for i in cr]
    s0 = [state_ref[bi, pi] for bi, pi, _ in ch]
    sk = [_dot_nt(lhs[i], _pair_diag(fe(i, 2).astype(BF16))) for i in cr]
    yield
    sb = [_dot_nt(lhs[i], _pair_diag(fe(i, 3).astype(BF16))) for i in cr]
    yield
    pa = [_dot_nt(lhs[i], _pair_diag(s0[i].astype(BF16))) for i in cr]
    yield
    rhs = [pa[i][:c] + _dot(jnp.where(strict, sk[i][:c], 0.0), vd[i]) for i in cr]
    yield
    u = yield from _unit_lower_solve_stages([jnp.where(strict, sb[i][:c], 0.0) for i in cr], rhs, pair=True)
    yield
    ya = [_dot(jnp.where(incl, sk[i][c:], 0.0), vd[i]) for i in cr]
    yield
    yb = [_pair_dot(jnp.where(incl, sb[i][c:], 0.0), u[i]) for i in cr]
    yield
    sa = [_dot_tn(fe(i, 6), fe(i, 4)) for i in cr]
    yield
    sc = [_dot_tn(u[i], fe(i, 5)) for i in cr]
    yield
    for i in cr:
        bi, pi, s = ch[i]
        yh_ref[bi - g0, :, s] = pa[i][c:] + ya[i] - yb[i]
        d = sa[i] - sc[i]
        state_ref[bi, pi] = s0[i] * egls[bi][:, s] + jnp.where(first_head, d[:hd], d[hd:])
    yield

    inv_n = 1.0 / hd
    for bi in seqs:
        y = yh_ref[bi - g0]
        mean = seg_sum(y) * inv_n
        dlt = y - mean
        var = seg_sum(dlt * dlt) * inv_n
        y = dlt * lax.rsqrt(var + RWKV_GN_EPS) * lng_ref[...] + lnb_ref[...]
        y_ref[bi] = ((y + fe_ref[bi - g0, 7]) * fe_ref[bi - g0, 8]).astype(y_ref.dtype)
        yield


def _head_sum(t, seg, segt):
    return _dot_exact_rhs(_dot_exact_rhs(t, seg, parts=1), segt)


def rwkv_mixer(p, shift0, s0, prm, heads, hd, lora_w, lora_a):
    b, t, cols = p.shape
    c = min(CHUNK, t)
    da = heads * hd
    seg = (jnp.arange(da)[:, None] // hd == jnp.arange(LANE)[None, :]).astype(BF16)
    segt = seg.T
    rowp = lambda a: a.reshape(1, -1)
    consts = [rowp(prm['mu_a']), rowp(prm['w0_a']), prm['w_w2'], rowp(prm['a0_a']), prm['w_a2'], prm['w_g2'],
              rowp(prm['k_k']), rowp(prm['k_a']), rowp(prm['r_k']), rowp(prm['lnx_g']), rowp(prm['lnx_b']),
              seg, segt]
    cspec = lambda a: pl.BlockSpec(a.shape, lambda i, j: (0, 0))
    bb, groups = _seq_groups(b, heads // 2)
    pair = lambda a: a.reshape(b, heads // 2, 2, hd, hd).swapaxes(2, 3).reshape(b, heads // 2, hd, 2 * hd)
    unpair = lambda a: a.reshape(b, heads // 2, hd, 2, hd).swapaxes(2, 3).reshape(b, heads, hd, hd)
    y, shift, s = pl.pallas_call(
        functools.partial(_rwkv_kernel, groups=groups, heads=heads, hd=hd, lora_w=lora_w, lora_a=lora_a),
        grid=(b // bb, t // c),
        in_specs=[pl.BlockSpec((bb, c, cols), lambda i, j: (i, j, 0)),
                  pl.BlockSpec((bb, 1, cols), lambda i, j: (i, 0, 0)),
                  pl.BlockSpec((bb, heads // 2, hd, 2 * hd), lambda i, j: (i, 0, 0, 0))] + [cspec(a) for a in consts],
        out_specs=[pl.BlockSpec((bb, c, da), lambda i, j: (i, j, 0)),
                   pl.BlockSpec((bb, 1, cols), lambda i, j: (i, 0, 0)),
                   pl.BlockSpec((bb, heads // 2, hd, 2 * hd), lambda i, j: (i, 0, 0, 0))],
        out_shape=[jax.ShapeDtypeStruct((b, t, da), BF16),
                   jax.ShapeDtypeStruct((b, 1, cols), F32),
                   jax.ShapeDtypeStruct((b, heads // 2, hd, 2 * hd), F32)],
        scratch_shapes=[pltpu.VMEM((bb, 1, cols), F32),
                        pltpu.VMEM((bb, heads // 2, hd, 2 * hd), F32)]
        + [pltpu.VMEM(shp, F32) for g in groups for shp in ((len(g), 9, c, da), (len(g), c, da))],
        compiler_params=_cparams("parallel", "arbitrary"),
    )(p, shift0.reshape(b, 1, cols), pair(s0), *consts)
    return y, shift.reshape(b, cols), unpair(s)


def _gla_kernel(p_ref, s0_ref, wgk_ref, bgk_ref, ng_ref, wseg_ref,
                y_ref, s_ref,
                state_ref, *, heads, dk, dv, lora):
    n = pl.program_id(1)
    bb, c, _ = p_ref.shape
    hk = heads * dk
    hv = heads * dv
    sub = min(SUB, c)
    nb = c // sub

    @pl.when(n == 0)
    def _():
        state_ref[...] = s0_ref[...]

    hr = range(heads)
    sr = range(bb)
    ksl = [slice(h * dk, (h + 1) * dk) for h in hr]
    vsl = [slice(h * dv, (h + 1) * dv) for h in hr]
    srow = lax.broadcasted_iota(jnp.int32, (sub, hk), 0)
    tril_bf = _tri(c, False).astype(BF16)
    wseg = wseg_ref[...]

    vs, gs, vbs, qgs, kds, egls, xss, qts, kts = [], [], [], [], [], [], [], [], []
    for si in sr:
        pf = p_ref[si]
        q = pf[:, 0:hk] * (dk ** -0.5)
        k = pf[:, hk:2 * hk]
        v = pf[:, 2 * hk:2 * hk + hv]
        gk = pf[:, 2 * hk + 2 * hv:2 * hk + 2 * hv + lora]
        z = _dot(gk, wgk_ref[...]) + bgk_ref[...]
        glog = (jnp.minimum(z, 0.0) - jnp.log(1.0 + jnp.exp(-jnp.abs(z)))) * (1.0 / GK_NORMALIZER)
        gc = _dot_exact_lhs(tril_bf, glog)
        gl = gc[c - 1:c, :]
        vs.append(v)
        gs.append(pf[:, 2 * hk + hv:2 * hk + 2 * hv])
        vbs.append(v.astype(BF16))
        qgs.append((q * jnp.exp(gc)).astype(BF16))
        kds.append((k * jnp.exp(gl - gc)).astype(BF16))
        egls.append(jnp.exp(gl))
        xs_blocks = []
        for bi in range(nb):
            r0 = bi * sub
            q_i, gc_i, k_i = q[r0:r0 + sub], gc[r0:r0 + sub], k[r0:r0 + sub]
            xs = []
            for jj in range(sub):
                e = jnp.exp(jnp.minimum(gc_i - gc_i[jj:jj + 1], 0.0))
                xs.append(jnp.where(srow >= jj, q_i * e * k_i[jj:jj + 1], 0.0).astype(BF16))
            xs_blocks.append(jnp.concatenate(xs, axis=0))
        xss.append(xs_blocks)
        qt, kt = [None], [None]
        for bi in range(1, nb):
            r0 = bi * sub
            ref_row = gc[r0:r0 + 1]
            qt.append((q[r0:r0 + sub] * jnp.exp(gc[r0:r0 + sub] - ref_row)).astype(BF16))
            kt.append((k[:r0] * jnp.exp(jnp.minimum(ref_row - gc[:r0], 0.0))).astype(BF16))
        qts.append(qt)
        kts.append(kt)

    s0 = [[state_ref[si, h] for h in hr] for si in sr]
    reds = [[jnp.dot(xss[si][bi], wseg, preferred_element_type=F32) for bi in range(nb)] for si in sr]
    o_in = [[_dot_nt(qgs[si][:, ksl[h]], s0[si][h]) for h in hr] for si in sr]
    sc = [[[_dot_nt(qts[si][bi][:, ksl[h]], kts[si][bi][:, ksl[h]]).astype(BF16) for h in hr]
           for bi in range(1, nb)] for si in sr]
    off = [[[_dot(sc[si][bi - 1][h], vbs[si][:bi * sub, vsl[h]]) for h in hr]
            for bi in range(1, nb)] for si in sr]
    s_up = [[_dot_tn(vbs[si][:, vsl[h]], kds[si][:, ksl[h]]) for h in hr] for si in sr]

    for si in sr:
        blocks = []
        for bi in range(nb):
            r0 = bi * sub
            v_i = vs[si][r0:r0 + sub]
            red = reds[si][bi]
            o_i = red[0:sub] * v_i[0:1]
            for jj in range(1, sub):
                o_i = o_i + red[jj * sub:(jj + 1) * sub] * v_i[jj:jj + 1]
            if bi > 0:
                o_i = o_i + jnp.concatenate(off[si][bi - 1], axis=-1)
            blocks.append(o_i)
        o = (jnp.concatenate(blocks, axis=0) if nb > 1 else blocks[0]) + jnp.concatenate(o_in[si], axis=-1)
        for h in hr:
            state_ref[si, h] = s0[si][h] * egls[si][:, ksl[h]] + s_up[si][h]
            o_h = _rms(o[:, vsl[h]], ng_ref[...])
            g_h = gs[si][:, vsl[h]]
            y_ref[si, :, vsl[h]] = (o_h * (g_h * _sigmoid(g_h))).astype(y_ref.dtype)

    @pl.when(n == pl.num_programs(1) - 1)
    def _():
        s_ref[...] = state_ref[...]


def gla_mixer(p, s0, prm, heads, dk, dv, lora):
    b, t, cols = p.shape
    c = min(CHUNK, t)
    hk, hv = heads * dk, heads * dv
    wseg = (jnp.arange(hk)[:, None] // dk == jnp.arange(hv)[None, :] // dv).astype(BF16)
    consts = [prm['w_gk2'], prm['b_gk'].reshape(1, hk), prm['gla_norm_g'].reshape(1, dv), wseg]
    cspec = lambda a: pl.BlockSpec(a.shape, lambda i, j: (0, 0))
    bb = _seq_block(b, heads)
    y, s = pl.pallas_call(
        functools.partial(_gla_kernel, heads=heads, dk=dk, dv=dv, lora=lora),
        grid=(b // bb, t // c),
        in_specs=[pl.BlockSpec((bb, c, cols), lambda i, j: (i, j, 0)),
                  pl.BlockSpec((bb, heads, dv, dk), lambda i, j: (i, 0, 0, 0))] + [cspec(a) for a in consts],
        out_specs=[pl.BlockSpec((bb, c, hv), lambda i, j: (i, j, 0)),
                   pl.BlockSpec((bb, heads, dv, dk), lambda i, j: (i, 0, 0, 0))],
        out_shape=[jax.ShapeDtypeStruct((b, t, hv), BF16),
                   jax.ShapeDtypeStruct((b, heads, dv, dk), F32)],
        scratch_shapes=[pltpu.VMEM((bb, heads, dv, dk), F32)],
        compiler_params=_cparams("parallel", "arbitrary"),
    )(p, jnp.swapaxes(s0, -1, -2), *consts)
    return y, jnp.swapaxes(s, -1, -2)


def _gdn_kernel(p_ref, conv0_ref, s0_ref, cw_ref, alog_ref, dtb_ref, ng_ref,
                y_ref, conv_ref, s_ref,
                xp_ref, state_ref, *, heads, dk, dv, conv_w):
    n = pl.program_id(1)
    bb, c, _ = p_ref.shape
    cq = 2 * heads * dk + heads * dv
    pad = 8

    @pl.when(n == 0)
    def _():
        xp_ref[:, 0:pad, :] = conv0_ref[...]
        state_ref[...] = s0_ref[...]

    tril = _tri(c, False)
    strict = _tri(c, True)
    tril_bf = tril.astype(BF16)
    triu_bf = (lax.broadcasted_iota(jnp.int32, (c, c), 0) <= lax.broadcasted_iota(jnp.int32, (c, c), 1)).astype(BF16)

    ch = [(si, h) for si in range(bb) for h in range(heads)]
    cr = range(len(ch))
    qs, qgs, ks, kds, kbs, rhss, decs, egl_last = [], [], [], [], [], [], [], []
    for si in range(bb):
        x = p_ref[si, :, 0:cq]
        tail = xp_ref[si, 0:pad, :]
        row8 = lax.broadcasted_iota(jnp.int32, (pad, cq), 0)
        conv = x * cw_ref[conv_w - 1:conv_w, :]
        for s in range(1, conv_w):
            xs = pltpu.roll(x, shift=s, axis=0)
            head = jnp.where(row8 < s, pltpu.roll(tail, shift=s, axis=0), xs[0:pad])
            xs = jnp.concatenate([head, xs[pad:]], axis=0) if c > pad else head
            conv = conv + xs * cw_ref[conv_w - 1 - s:conv_w - s, :]
        xp_ref[si, 0:pad, :] = x[c - pad:c]
        qkv = conv * _sigmoid(conv)

        ab = p_ref[si, :, cq + heads * dv:]
        beta = _sigmoid(ab)
        g = -jnp.exp(alog_ref[...]) * _softplus(ab + dtb_ref[...])
        gc = _dot_exact_lhs(tril_bf, g)
        gct = None
        for part in _split(g, 3):
            d = lax.dot_general(part, triu_bf, (((0,), (0,)), ((), ())), preferred_element_type=F32)
            gct = d if gct is None else gct + d
        gl = gc[c - 1:c, :]
        eg = jnp.exp(gc)
        egl = jnp.exp(gl - gc)
        egl_last.append(jnp.exp(gl))
        for h in range(heads):
            q = qkv[:, h * dk:(h + 1) * dk]
            k = qkv[:, heads * dk + h * dk:heads * dk + (h + 1) * dk]
            v = qkv[:, 2 * heads * dk + h * dv:2 * heads * dk + (h + 1) * dv]
            q = q * lax.rsqrt(jnp.sum(q * q, axis=-1, keepdims=True) + EPS) * (dk ** -0.5)
            k = k * lax.rsqrt(jnp.sum(k * k, axis=-1, keepdims=True) + EPS)
            bcol = beta[:, heads + h:heads + h + 1]
            diff = gc[:, h:h + 1] - gct[h:h + 1, :]
            decs.append(jnp.where(tril, jnp.exp(jnp.where(tril, diff, 0.0)), 0.0))
            kb = k * bcol
            qs.append(q.astype(BF16))
            qgs.append((q * eg[:, h:h + 1]).astype(BF16))
            ks.append(k.astype(BF16))
            kds.append((k * egl[:, h:h + 1]).astype(BF16))
            kbs.append(kb.astype(BF16))
            rhss.append(jnp.concatenate([v * bcol, kb * eg[:, h:h + 1]], axis=-1))
    lmat = [jnp.where(strict, _dot_nt(kbs[i], ks[i]) * decs[i], 0.0) for i in cr]
    att = [(_dot_nt(qs[i], ks[i]) * decs[i]).astype(BF16) for i in cr]
    uw = _unit_lower_solve(lmat, rhss)
    s0 = [state_ref[si, h] for si, h in ch]
    s0b = [s.astype(BF16) for s in s0]
    v_new = [uw[i][:, :dv] - _dot(uw[i][:, dv:], s0b[i]) for i in cr]
    o_in = [_dot(qgs[i], s0b[i]) for i in cr]
    o_at = [_dot(att[i], v_new[i]) for i in cr]
    s_up = [_dot_tn(kds[i], v_new[i]) for i in cr]
    for i in cr:
        si, h = ch[i]
        state_ref[si, h] = s0[i] * egl_last[si][:, h:h + 1] + s_up[i]
        o = _rms(o_in[i] + o_at[i], ng_ref[...])
        z_h = p_ref[si, :, cq + h * dv:cq + (h + 1) * dv]
        y_ref[si, :, h * dv:(h + 1) * dv] = (o * (z_h * _sigmoid(z_h))).astype(y_ref.dtype)

    @pl.when(n == pl.num_programs(1) - 1)
    def _():
        conv_ref[...] = xp_ref[:, 0:pad, :]
        s_ref[...] = state_ref[...]


def gdn_mixer(p, conv0, s0, prm, heads, dk, dv, conv_w):
    b, t, cols = p.shape
    c = min(CHUNK, t)
    cq = 2 * heads * dk + heads * dv
    pad = 8
    lanep = lambda a: jnp.pad(a.reshape(1, -1), ((0, 0), (0, LANE - a.size)))
    conv0p = jnp.pad(conv0, ((0, 0), (pad - (conv_w - 1), 0), (0, 0)))
    cw = jnp.pad(prm['conv_w'], ((0, pad - conv_w), (0, 0)))
    consts = [cw, lanep(prm['a_log']), lanep(prm['dt_bias']), prm['gdn_norm_g'].reshape(1, dv)]
    cspec = lambda a: pl.BlockSpec(a.shape, lambda i, j: (0, 0))
    bb = _seq_block(b, heads)
    y, conv, s = pl.pallas_call(
        functools.partial(_gdn_kernel, heads=heads, dk=dk, dv=dv, conv_w=conv_w),
        grid=(b // bb, t // c),
        in_specs=[pl.BlockSpec((bb, c, cols), lambda i, j: (i, j, 0)),
                  pl.BlockSpec((bb, pad, cq), lambda i, j: (i, 0, 0)),
                  pl.BlockSpec((bb, heads, dk, dv), lambda i, j: (i, 0, 0, 0))] + [cspec(a) for a in consts],
        out_specs=[pl.BlockSpec((bb, c, heads * dv), lambda i, j: (i, j, 0)),
                   pl.BlockSpec((bb, pad, cq), lambda i, j: (i, 0, 0)),
                   pl.BlockSpec((bb, heads, dk, dv), lambda i, j: (i, 0, 0, 0))],
        out_shape=[jax.ShapeDtypeStruct((b, t, heads * dv), BF16),
                   jax.ShapeDtypeStruct((b, pad, cq), F32),
                   jax.ShapeDtypeStruct((b, heads, dk, dv), F32)],
        scratch_shapes=[pltpu.VMEM((bb, pad, cq), F32),
                        pltpu.VMEM((bb, heads, dk, dv), F32)],
        compiler_params=_cparams("parallel", "arbitrary"),
    )(p, conv0p, s0, *consts)
    return y, conv[:, pad - (conv_w - 1):, :], s


D_MODEL = 2048
DEPTH = 2
H_A, N_A = 12, 64
D_A = H_A * N_A
LORA_W, LORA_A, LORA_G = 64, 64, 128
H_B, DK_B, DV_B, LORA_GK = 4, 64, 128, 16
H_C, DK_C, DV_C, CONV_W = 6, 128, 128, 4
A_COLS = 3 * D_A + LORA_W + LORA_A + LORA_G
B_COLS = 2 * H_B * DK_B + 2 * H_B * DV_B + LORA_GK
C_QKV = 2 * H_C * DK_C + H_C * DV_C
C_COLS = C_QKV + H_C * DV_C + 2 * H_C
H_X = 4


def _row_tile(m, cap):
    t = min(m, cap)
    while m % t:
        t //= 2
    return t


def _trunk(x, mem_k, mem_v, st_shift, st_rwkv, st_gla, st_conv, st_gdn, lw, final_g):
    b, t, d = x.shape
    m = b * t
    xf = x.reshape(m, d)
    tm_s = _row_tile(m, 512)
    tq = _row_tile(t, 512)
    outs = [[] for _ in range(5)]
    for l in range(DEPTH):
        w = lw[l]
        proj = lambda wi: norm_matmul(xf, w['norm1_g'], wi, tm_s, wi.shape[1]).reshape(b, t, -1)
        ya, sh, sa = rwkv_mixer(proj(w['w_in_a']), st_shift[l], st_rwkv[l], w, H_A, N_A, LORA_W, LORA_A)
        yb, sb = gla_mixer(proj(w['w_in_b']), st_gla[l], w, H_B, DK_B, DV_B, LORA_GK)
        yc, cb, sc = gdn_mixer(proj(w['w_in_c']), st_conv[l], st_gdn[l], w, H_C, DK_C, DV_C, CONV_W)
        x1 = out_proj(xf, ya.reshape(m, -1), yb.reshape(m, -1), yc.reshape(m, -1),
                      w['w_out_a'], w['w_out_b'], w['w_out_c'], tm_s)
        x2 = cross_attn(x1.reshape(b, t, d), w['normx_g'], w['wq_x'], w['wo_x'], mem_k[l], mem_v[l], H_X, tq)
        xf = mlp(x2.reshape(m, d), w['norm3_g'], final_g, w['w_up'], w['w_down'], tm_s, 1024,
                 final_norm=(l == DEPTH - 1))
        for lst, val in zip(outs, (sh, sa, sb, cb, sc)):
            lst.append(val)
    return (xf.reshape(b, t, d),) + tuple(jnp.stack(o) for o in outs)


def kernel(x_prompt, x_sample, state_rwkv_shift, state_rwkv, state_gla, state_gdn_conv, state_gdn, cache_mem_k, cache_mem_v, mem_prompt, norm1_g, w_in, mu_a, w0_a, w_w2, a0_a, w_a2, w_g2, k_k, k_a, r_k, lnx_g, lnx_b, w_gk2, b_gk, gla_norm_g, conv_w, a_log, dt_bias, gdn_norm_g, w_out, normx_g, normm_g, wq_x, wk_x, wv_x, wo_x, norm3_g, w_up, w_down, final_g):
    bf = lambda a: a.astype(BF16)
    w_in_a, w_in_b, w_in_c = split_cols(w_in, A_COLS, B_COLS, 256)
    lw = []
    for l in range(DEPTH):
        lw.append(dict(
            norm1_g=norm1_g[l],
            w_in_a=w_in_a[l], w_in_b=w_in_b[l], w_in_c=w_in_c[l],
            mu_a=mu_a[l], w0_a=w0_a[l], w_w2=w_w2[l], a0_a=a0_a[l], w_a2=w_a2[l], w_g2=w_g2[l],
            k_k=k_k[l], k_a=k_a[l], r_k=r_k[l], lnx_g=lnx_g[l], lnx_b=lnx_b[l],
            w_gk2=w_gk2[l], b_gk=b_gk[l], gla_norm_g=gla_norm_g[l],
            conv_w=conv_w[l], a_log=a_log[l], dt_bias=dt_bias[l], gdn_norm_g=gdn_norm_g[l],
            w_out_a=bf(w_out[l, :D_A]), w_out_b=bf(w_out[l, D_A:D_A + H_B * DV_B]),
            w_out_c=bf(w_out[l, D_A + H_B * DV_B:]),
            normx_g=normx_g[l], wq_x=bf(wq_x[l]), wo_x=bf(wo_x[l]),
            norm3_g=norm3_g[l], w_up=bf(w_up[l]), w_down=bf(w_down[l])))

    b, n_mem, d = mem_prompt.shape
    memf = mem_prompt.reshape(b * n_mem, d)
    hd_x = d // H_X
    tmm = _row_tile(b * n_mem, 1024)
    p_mem_k = jnp.stack([norm_matmul(memf, normm_g[l], bf(wk_x[l]), tmm, 512).reshape(b, n_mem, H_X, hd_x)
                         for l in range(DEPTH)])
    p_mem_v = jnp.stack([norm_matmul(memf, normm_g[l], bf(wv_x[l]), tmm, 512).reshape(b, n_mem, H_X, hd_x)
                         for l in range(DEPTH)])

    zeros = lambda *s: jnp.zeros((DEPTH, b) + s, F32)
    flat_mem = lambda a: bf(a.reshape(a.shape[0], a.shape[1], n_mem, d))
    y_prompt, p_shift, p_rwkv, p_gla, p_conv, p_gdn = _trunk(
        x_prompt, flat_mem(p_mem_k), flat_mem(p_mem_v),
        zeros(A_COLS), zeros(H_A, N_A, N_A), zeros(H_B, DK_B, DV_B), zeros(CONV_W - 1, C_QKV),
        zeros(H_C, DK_C, DV_C), lw, final_g)
    y_sample, s_shift, s_rwkv, s_gla, s_conv, s_gdn = _trunk(
        x_sample, flat_mem(cache_mem_k), flat_mem(cache_mem_v),
        state_rwkv_shift, state_rwkv, state_gla, state_gdn_conv, state_gdn, lw, final_g)
    return (y_prompt, y_sample, p_shift, p_rwkv, p_gla, p_conv, p_gdn, p_mem_k, p_mem_v,
            s_shift, s_rwkv, s_gla, s_conv, s_gdn)
```

```python
import functools

import jax
import jax.numpy as jnp
from jax import lax
from jax.experimental import pallas as pl
from jax.experimental.pallas import tpu as pltpu

F32 = jnp.float32
BF16 = jnp.bfloat16

EPS = 1e-6
RWKV_GN_EPS = 64e-5
GK_NORMALIZER = 16.0
CHUNK = 64
SUB = 16
LANE = 128
VMEM_LIMIT = 56 * 1024 * 1024


def _cparams(*sem):
    return pltpu.CompilerParams(dimension_semantics=sem, vmem_limit_bytes=VMEM_LIMIT)


MIN_CHAINS = 24


def _seq_block(b, heads):
    for d in range(1, b + 1):
        if b % d == 0 and d * heads >= MIN_CHAINS:
            return d
    return b


def _seq_groups(b, heads):
    g = _seq_block(b, heads)
    return g, (tuple(range(g)),)


def _dot(a, b):
    return jnp.dot(a.astype(BF16), b.astype(BF16), preferred_element_type=F32)


def _dot_nt(a, b):
    return lax.dot_general(a.astype(BF16), b.astype(BF16), (((1,), (1,)), ((), ())),
                           preferred_element_type=F32)


def _dot_tn(a, b):
    return lax.dot_general(a.astype(BF16), b.astype(BF16), (((0,), (0,)), ((), ())),
                           preferred_element_type=F32)


def _split(x, parts):
    out = []
    r = x
    for _ in range(parts):
        p = r.astype(BF16)
        out.append(p)
        r = r - p.astype(F32)
    return out


def _dot_exact_lhs(l_bf, x, parts=3):
    acc = None
    for p in _split(x, parts):
        d = jnp.dot(l_bf, p, preferred_element_type=F32)
        acc = d if acc is None else acc + d
    return acc


def _dot_exact_rhs(x, r_bf, parts=2):
    acc = None
    for p in _split(x, parts):
        d = jnp.dot(p, r_bf, preferred_element_type=F32)
        acc = d if acc is None else acc + d
    return acc


def _rms(x, g):
    ms = jnp.mean(x * x, axis=-1, keepdims=True)
    return x * lax.rsqrt(ms + EPS) * g


def _sigmoid(x):
    return 1.0 / (1.0 + jnp.exp(-x))


def _softplus(x):
    return jnp.maximum(x, 0.0) + jnp.log(1.0 + jnp.exp(-jnp.abs(x)))


def _tri(c, strict):
    r = lax.broadcasted_iota(jnp.int32, (c, c), 0)
    q = lax.broadcasted_iota(jnp.int32, (c, c), 1)
    return (r > q) if strict else (r >= q)


INV_BASE = 8


def _unit_lower_solve(ls, rhss):
    return _drain(_unit_lower_solve_stages(ls, rhss))


def _pair_diag(x):
    first = lax.broadcasted_iota(jnp.int32, x.shape, 1) < x.shape[1] // 2
    zero = jnp.zeros_like(x)
    return jnp.concatenate([jnp.where(first, x, zero), jnp.where(first, zero, x)], axis=0)


def _pair_dot(a, b):
    return _dot(a, _pair_diag(b.astype(BF16)))


def _unit_lower_solve_stages(ls, rhss, pair=False):
    c = ls[0].shape[0]
    mm = _pair_dot if pair else _dot
    r = lax.broadcasted_iota(jnp.int32, ls[0].shape, 0)
    q = lax.broadcasted_iota(jnp.int32, ls[0].shape, 1) % c
    eye = (r == q).astype(F32)
    base = min(INV_BASE, c)
    diag_blk = (r // base) == (q // base)
    ns = [jnp.where(diag_blk, -l, 0.0).astype(BF16) for l in ls]
    ts = [eye + n.astype(F32) for n in ns]
    p = 2
    while p < base:
        ns = [mm(n, n).astype(BF16) for n in ns]
        yield
        ts = [t + mm(t, n) for t, n in zip(ts, ns)]
        yield
        p *= 2
    s = base
    while s < c:
        sub_blk = ((r // s) % 2 == 1) & ((q // s) == (r // s) - 1)
        cs = [jnp.where(sub_blk, l, 0.0).astype(BF16) for l in ls]
        tb = [t.astype(BF16) for t in ts]
        cts = [mm(cm, t).astype(BF16) for cm, t in zip(cs, tb)]
        yield
        ts = [t - mm(b, ct) for t, b, ct in zip(ts, tb, cts)]
        yield
        s *= 2
    return [mm(t, rh) for t, rh in zip(ts, rhss)]


def _drain(gen):
    try:
        while True:
            next(gen)
    except StopIteration as stop:
        return stop.value


def _interleave(*gens):
    live = list(gens)
    while live:
        for g in list(live):
            try:
                next(g)
            except StopIteration:
                live.remove(g)


def _norm_matmul_kernel(x_ref, g_ref, w_ref, o_ref, h_ref):
    @pl.when(pl.program_id(1) == 0)
    def _():
        h_ref[...] = _rms(x_ref[...], g_ref[...]).astype(BF16)

    o_ref[...] = jnp.dot(h_ref[...], w_ref[...], preferred_element_type=F32).astype(o_ref.dtype)


def norm_matmul(x, g, w, tm, tn):
    m, k = x.shape
    n = w.shape[1]
    return pl.pallas_call(
        _norm_matmul_kernel,
        grid=(m // tm, n // tn),
        in_specs=[pl.BlockSpec((tm, k), lambda i, j: (i, 0)),
                  pl.BlockSpec((1, k), lambda i, j: (0, 0)),
                  pl.BlockSpec((k, tn), lambda i, j: (0, j))],
        out_specs=pl.BlockSpec((tm, tn), lambda i, j: (i, j)),
        out_shape=jax.ShapeDtypeStruct((m, n), F32),
        scratch_shapes=[pltpu.VMEM((tm, k), BF16)],
        compiler_params=_cparams("parallel", "arbitrary"),
    )(x, g.reshape(1, k), w)


def _split_cols_kernel(w_ref, a_ref, b_ref, c_ref, *, na, nb):
    n = w_ref.shape[2]
    nc = n - na - nb
    a_ref[0] = w_ref[0, :, 0:na].astype(BF16)
    b_ref[0, :, 0:nb] = w_ref[0, :, na:na + nb].astype(BF16)
    b_ref[0, :, nb:] = jnp.zeros((b_ref.shape[1], b_ref.shape[2] - nb), BF16)
    c_ref[0, :, 0:nc] = w_ref[0, :, na + nb:].astype(BF16)
    c_ref[0, :, nc:] = jnp.zeros((c_ref.shape[1], c_ref.shape[2] - nc), BF16)


def split_cols(w, na, nb, tk):
    depth, k, n = w.shape
    nc = n - na - nb
    widths = [na, nb + (-nb % LANE), nc + (-nc % LANE)]
    return pl.pallas_call(
        functools.partial(_split_cols_kernel, na=na, nb=nb),
        grid=(depth, k // tk),
        in_specs=[pl.BlockSpec((1, tk, n), lambda l, i: (l, i, 0))],
        out_specs=[pl.BlockSpec((1, tk, wd), lambda l, i: (l, i, 0)) for wd in widths],
        out_shape=[jax.ShapeDtypeStruct((depth, k, wd), BF16) for wd in widths],
        compiler_params=_cparams("parallel", "parallel"),
    )(w)


def _cross_attn_kernel(x_ref, ya_ref, yb_ref, yc_ref, wa_ref, wb_ref, wc_ref,
                       g_ref, wq_ref, wo_ref, k_ref, v_ref, o_ref, *, heads):
    x = x_ref[0] + jnp.dot(ya_ref[0], wa_ref[...], preferred_element_type=F32)
    x = x + jnp.dot(yb_ref[0], wb_ref[...], preferred_element_type=F32)
    x = x + jnp.dot(yc_ref[0], wc_ref[...], preferred_element_type=F32)
    d = x.shape[-1]
    hd = d // heads
    h = _rms(x, g_ref[...]).astype(BF16)
    q = jnp.dot(h, wq_ref[...], preferred_element_type=F32).astype(BF16)
    outs = []
    for i in range(heads):
        hs = slice(i * hd, (i + 1) * hd)
        s = _dot_nt(q[:, hs], k_ref[0, :, hs]) * (hd ** -0.5)
        s = s - jnp.max(s, axis=-1, keepdims=True)
        e = jnp.exp(s)
        pr = e / jnp.sum(e, axis=-1, keepdims=True)
        outs.append(jnp.dot(pr.astype(BF16), v_ref[0, :, hs], preferred_element_type=F32).astype(BF16))
    o = jnp.concatenate(outs, axis=-1)
    o_ref[0] = x + jnp.dot(o, wo_ref[...], preferred_element_type=F32)


def cross_attn(x, ya, yb, yc, wa, wb, wc, g, wq, wo, mk, mv, heads, tq):
    b, t, d = x.shape
    n_mem = mk.shape[1]
    full = lambda a: pl.BlockSpec(a.shape, lambda i, j: (0, 0), pipeline_mode=pl.Buffered(1))
    rows = lambda a: pl.BlockSpec((1, tq, a.shape[2]), lambda i, j: (i, j, 0))
    return pl.pallas_call(
        functools.partial(_cross_attn_kernel, heads=heads),
        grid=(b, t // tq),
        in_specs=[rows(x), rows(ya), rows(yb), rows(yc), full(wa), full(wb), full(wc),
                  pl.BlockSpec((1, d), lambda i, j: (0, 0)),
                  full(wq), full(wo),
                  pl.BlockSpec((1, n_mem, d), lambda i, j: (i, 0, 0)),
                  pl.BlockSpec((1, n_mem, d), lambda i, j: (i, 0, 0))],
        out_specs=rows(x),
        out_shape=jax.ShapeDtypeStruct((b, t, d), F32),
        compiler_params=_cparams("parallel", "parallel"),
    )(x, ya, yb, yc, wa, wb, wc, g.reshape(1, d), wq, wo, mk, mv)


def _mlp_kernel(x_ref, g_ref, fg_ref, wu_ref, wd_ref, o_ref, h_ref, acc_ref, *, final_norm):
    f = pl.program_id(1)

    @pl.when(f == 0)
    def _():
        h_ref[...] = _rms(x_ref[...], g_ref[...]).astype(BF16)
        acc_ref[...] = jnp.zeros_like(acc_ref)

    u = jnp.maximum(jnp.dot(h_ref[...], wu_ref[...], preferred_element_type=F32), 0.0)
    acc_ref[...] += jnp.dot((u * u).astype(BF16), wd_ref[...], preferred_element_type=F32)

    @pl.when(f == pl.num_programs(1) - 1)
    def _():
        y = x_ref[...] + acc_ref[...]
        if final_norm:
            y = _rms(y, fg_ref[...])
        o_ref[...] = y


def mlp(x, g, fg, wu, wd, tm, tf, final_norm):
    m, d = x.shape
    dff = wu.shape[1]
    return pl.pallas_call(
        functools.partial(_mlp_kernel, final_norm=final_norm),
        grid=(m // tm, dff // tf),
        in_specs=[pl.BlockSpec((tm, d), lambda i, j: (i, 0)),
                  pl.BlockSpec((1, d), lambda i, j: (0, 0)),
                  pl.BlockSpec((1, d), lambda i, j: (0, 0)),
                  pl.BlockSpec((d, tf), lambda i, j: (0, j)),
                  pl.BlockSpec((tf, d), lambda i, j: (j, 0))],
        out_specs=pl.BlockSpec((tm, d), lambda i, j: (i, 0)),
        out_shape=jax.ShapeDtypeStruct((m, d), F32),
        scratch_shapes=[pltpu.VMEM((tm, d), BF16), pltpu.VMEM((tm, d), F32)],
        compiler_params=_cparams("parallel", "arbitrary"),
    )(x, g.reshape(1, d), fg.reshape(1, d), wu, wd)


def _rwkv_kernel(p_ref, shift0_ref, s0_ref, *rest, groups, **dims):
    n = pl.program_id(1)
    scr = rest[-2 * len(groups):]
    base = rest[:-2 * len(groups)]
    shift_ref, s_ref, carry_ref, state_ref = base[-4:]

    @pl.when(n == 0)
    def _():
        carry_ref[...] = shift0_ref[...]
        state_ref[...] = s0_ref[...]

    egls = {}
    front = [_rwkv_front(seqs, egls, p_ref, *base, scr[2 * gi], scr[2 * gi + 1], **dims)
             for gi, seqs in enumerate(groups)]
    chain = [_rwkv_chain(seqs, egls, p_ref, *base, scr[2 * gi], scr[2 * gi + 1], **dims)
             for gi, seqs in enumerate(groups)]
    _drain(front[0])
    for gi in range(1, len(groups)):
        _interleave(chain[gi - 1], front[gi])
    _drain(chain[-1])

    @pl.when(n == pl.num_programs(1) - 1)
    def _():
        shift_ref[...] = carry_ref[...]
        s_ref[...] = state_ref[...]


def _rwkv_front(seqs, egls, p_ref, mu_ref, w0_ref, ww2_ref, a0_ref, wa2_ref, wg2_ref,
                kk_ref, ka_ref, rk_ref, lng_ref, lnb_ref, seg_ref, segt_ref,
                y_ref, shift_ref, s_ref,
                carry_ref, state_ref, fe_ref, yh_ref, *, heads, hd, lora_w, lora_a):
    c = p_ref.shape[1]
    da = heads * hd
    g0 = seqs[0]

    seg = seg_ref[...]
    segt = segt_ref[...]
    seg_sum = lambda t: _head_sum(t, seg, segt)
    tril_bf = _tri(c, False).astype(BF16)

    for bi in seqs:
        pf = p_ref[bi]
        row = lax.broadcasted_iota(jnp.int32, pf.shape, 0)
        prev = jnp.where(row == 0, carry_ref[bi], pltpu.roll(pf, shift=1, axis=0))
        carry_ref[bi] = pf[c - 1:c, :]
        ps = pf + (prev - pf) * mu_ref[...]

        r = ps[:, 0:da]
        k = ps[:, da:2 * da]
        v = ps[:, 2 * da:3 * da]
        o = 3 * da
        xw = ps[:, o:o + lora_w]
        xa = ps[:, o + lora_w:o + lora_w + lora_a]
        xg = ps[:, o + lora_w + lora_a:]

        w_log = -_softplus(-(w0_ref[...] + _dot(jnp.tanh(xw), ww2_ref[...]))) - 0.5
        ld = -jnp.exp(w_log)
        a = _sigmoid(a0_ref[...] + _dot(xa, wa2_ref[...]))
        fe_ref[bi - g0, 8] = _dot(_sigmoid(xg), wg2_ref[...])
        yield

        kraw = k * kk_ref[...]
        kk = kraw * lax.rsqrt(seg_sum(kraw * kraw) + EPS)
        k2 = k * (1.0 + (a - 1.0) * ka_ref[...])
        b = kk * a
        yield

        gam = _dot_exact_lhs(tril_bf, ld)
        gl = gam[c - 1:c, :]
        eng = jnp.exp(-gam)
        el = jnp.exp(gl - gam)
        fe_ref[bi - g0, 0] = kk * jnp.exp(gam - ld)
        fe_ref[bi - g0, 1] = r * jnp.exp(gam)
        fe_ref[bi - g0, 2] = k2 * eng
        fe_ref[bi - g0, 3] = b * eng
        yield
        fe_ref[bi - g0, 4] = k2 * el
        fe_ref[bi - g0, 5] = b * el
        fe_ref[bi - g0, 6] = v
        fe_ref[bi - g0, 7] = seg_sum(r * k2 * rk_ref[...]) * v
        egls[bi] = jnp.exp(gl)
        yield


def _rwkv_chain(seqs, egls, p_ref, mu_ref, w0_ref, ww2_ref, a0_ref, wa2_ref, wg2_ref,
                kk_ref, ka_ref, rk_ref, lng_ref, lnb_ref, seg_ref, segt_ref,
                y_ref, shift_ref, s_ref,
                carry_ref, state_ref, fe_ref, yh_ref, *, heads, hd, lora_w, lora_a):
    c = p_ref.shape[1]
    g0 = seqs[0]
    seg = seg_ref[...]
    segt = segt_ref[...]
    seg_sum = lambda t: _head_sum(t, seg, segt)
    w = 2 * hd
    row = lax.broadcasted_iota(jnp.int32, (c, 2 * c), 0)
    col = lax.broadcasted_iota(jnp.int32, (c, 2 * c), 1) % c
    strict = row > col
    incl = row >= col
    first_head = lax.broadcasted_iota(jnp.int32, (hd, w), 1) < hd
    ch = [(bi, pi, slice(pi * w, (pi + 1) * w)) for bi in seqs for pi in range(heads // 2)]
    cr = range(len(ch))
    fe = lambda i, k: fe_ref[ch[i][0] - g0, k, :, ch[i][2]]
    lhs = [jnp.concatenate([fe(i, 0), fe(i, 1)], axis=0).astype(BF16) for i in cr]
    vd = [_pair_diag(fe(i, 6).astype(BF16)) for i in cr]
    s0 = [state_ref[bi, pi] for bi, pi, _ in ch]
    sk = [_dot_nt(lhs[i], _pair_diag(fe(i, 2).astype(BF16))) for i in cr]
    yield
    sb = [_dot_nt(lhs[i], _pair_diag(fe(i, 3).astype(BF16))) for i in cr]
    yield
    pa = [_dot_nt(lhs[i], _pair_diag(s0[i].astype(BF16))) for i in cr]
    yield
    rhs = [pa[i][:c] + _dot(jnp.where(strict, sk[i][:c], 0.0), vd[i]) for i in cr]
    yield
    u = yield from _unit_lower_solve_stages([jnp.where(strict, sb[i][:c], 0.0) for i in cr], rhs, pair=True)
    yield
    ya = [_dot(jnp.where(incl, sk[i][c:], 0.0), vd[i]) for i in cr]
    yield
    yb = [_pair_dot(jnp.where(incl, sb[i][c:], 0.0), u[i]) for i in cr]
    yield
    sa = [_dot_tn(fe(i, 6), fe(i, 4)) for i in cr]
    yield
    sc = [_dot_tn(u[i], fe(i, 5)) for i in cr]
    yield
    for i in cr:
        bi, pi, s = ch[i]
        yh_ref[bi - g0, :, s] = pa[i][c:] + ya[i] - yb[i]
        d = sa[i] - sc[i]
        state_ref[bi, pi] = s0[i] * egls[bi][:, s] + jnp.where(first_head, d[:hd], d[hd:])
    yield

    inv_n = 1.0 / hd
    for bi in seqs:
        y = yh_ref[bi - g0]
        mean = seg_sum(y) * inv_n
        dlt = y - mean
        var = seg_sum(dlt * dlt) * inv_n
        y = dlt * lax.rsqrt(var + RWKV_GN_EPS) * lng_ref[...] + lnb_ref[...]
        y_ref[bi] = ((y + fe_ref[bi - g0, 7]) * fe_ref[bi - g0, 8]).astype(y_ref.dtype)
        yield


def _head_sum(t, seg, segt):
    return _dot_exact_rhs(_dot_exact_rhs(t, seg, parts=1), segt)


def rwkv_mixer(p, shift0, s0, prm, heads, hd, lora_w, lora_a):
    b, t, cols = p.shape
    c = min(CHUNK, t)
    da = heads * hd
    seg = (jnp.arange(da)[:, None] // hd == jnp.arange(LANE)[None, :]).astype(BF16)
    segt = seg.T
    rowp = lambda a: a.reshape(1, -1)
    consts = [rowp(prm['mu_a']), rowp(prm['w0_a']), prm['w_w2'], rowp(prm['a0_a']), prm['w_a2'], prm['w_g2'],
              rowp(prm['k_k']), rowp(prm['k_a']), rowp(prm['r_k']), rowp(prm['lnx_g']), rowp(prm['lnx_b']),
              seg, segt]
    cspec = lambda a: pl.BlockSpec(a.shape, lambda i, j: (0, 0))
    bb, groups = _seq_groups(b, heads // 2)
    pair = lambda a: a.reshape(b, heads // 2, 2, hd, hd).swapaxes(2, 3).reshape(b, heads // 2, hd, 2 * hd)
    unpair = lambda a: a.reshape(b, heads // 2, hd, 2, hd).swapaxes(2, 3).reshape(b, heads, hd, hd)
    y, shift, s = pl.pallas_call(
        functools.partial(_rwkv_kernel, groups=groups, heads=heads, hd=hd, lora_w=lora_w, lora_a=lora_a),
        grid=(b // bb, t // c),
        in_specs=[pl.BlockSpec((bb, c, cols), lambda i, j: (i, j, 0)),
                  pl.BlockSpec((bb, 1, cols), lambda i, j: (i, 0, 0)),
                  pl.BlockSpec((bb, heads // 2, hd, 2 * hd), lambda i, j: (i, 0, 0, 0))] + [cspec(a) for a in consts],
        out_specs=[pl.BlockSpec((bb, c, da), lambda i, j: (i, j, 0)),
                   pl.BlockSpec((bb, 1, cols), lambda i, j: (i, 0, 0)),
                   pl.BlockSpec((bb, heads // 2, hd, 2 * hd), lambda i, j: (i, 0, 0, 0))],
        out_shape=[jax.ShapeDtypeStruct((b, t, da), BF16),
                   jax.ShapeDtypeStruct((b, 1, cols), F32),
                   jax.ShapeDtypeStruct((b, heads // 2, hd, 2 * hd), F32)],
        scratch_shapes=[pltpu.VMEM((bb, 1, cols), F32),
                        pltpu.VMEM((bb, heads // 2, hd, 2 * hd), F32)]
        + [pltpu.VMEM(shp, F32) for g in groups for shp in ((len(g), 9, c, da), (len(g), c, da))],
        compiler_params=_cparams("parallel", "arbitrary"),
    )(p, shift0.reshape(b, 1, cols), pair(s0), *consts)
    return y, shift.reshape(b, cols), unpair(s)


def _gla_kernel(p_ref, s0_ref, wgk_ref, bgk_ref, ng_ref, wseg_ref,
                y_ref, s_ref,
                state_ref, *, heads, dk, dv, lora):
    n = pl.program_id(1)
    bb, c, _ = p_ref.shape
    hk = heads * dk
    hv = heads * dv
    sub = min(SUB, c)
    nb = c // sub

    @pl.when(n == 0)
    def _():
        state_ref[...] = s0_ref[...]

    hr = range(heads)
    sr = range(bb)
    ksl = [slice(h * dk, (h + 1) * dk) for h in hr]
    vsl = [slice(h * dv, (h + 1) * dv) for h in hr]
    srow = lax.broadcasted_iota(jnp.int32, (sub, hk), 0)
    tril_bf = _tri(c, False).astype(BF16)
    wseg = wseg_ref[...]

    vs, gs, vbs, qgs, kds, egls, xss, qts, kts = [], [], [], [], [], [], [], [], []
    for si in sr:
        pf = p_ref[si]
        q = pf[:, 0:hk] * (dk ** -0.5)
        k = pf[:, hk:2 * hk]
        v = pf[:, 2 * hk:2 * hk + hv]
        gk = pf[:, 2 * hk + 2 * hv:2 * hk + 2 * hv + lora]
        z = _dot(gk, wgk_ref[...]) + bgk_ref[...]
        glog = (jnp.minimum(z, 0.0) - jnp.log(1.0 + jnp.exp(-jnp.abs(z)))) * (1.0 / GK_NORMALIZER)
        gc = _dot_exact_lhs(tril_bf, glog)
        gl = gc[c - 1:c, :]
        vs.append(v)
        gs.append(pf[:, 2 * hk + hv:2 * hk + 2 * hv])
        vbs.append(v.astype(BF16))
        qgs.append((q * jnp.exp(gc)).astype(BF16))
        kds.append((k * jnp.exp(gl - gc)).astype(BF16))
        egls.append(jnp.exp(gl))
        xs_blocks = []
        for bi in range(nb):
            r0 = bi * sub
            q_i, gc_i, k_i = q[r0:r0 + sub], gc[r0:r0 + sub], k[r0:r0 + sub]
            xs = []
            for jj in range(sub):
                e = jnp.exp(jnp.minimum(gc_i - gc_i[jj:jj + 1], 0.0))
                xs.append(jnp.where(srow >= jj, q_i * e * k_i[jj:jj + 1], 0.0).astype(BF16))
            xs_blocks.append(jnp.concatenate(xs, axis=0))
        xss.append(xs_blocks)
        qt, kt = [None], [None]
        for bi in range(1, nb):
            r0 = bi * sub
            ref_row = gc[r0:r0 + 1]
            qt.append((q[r0:r0 + sub] * jnp.exp(gc[r0:r0 + sub] - ref_row)).astype(BF16))
            kt.append((k[:r0] * jnp.exp(jnp.minimum(ref_row - gc[:r0], 0.0))).astype(BF16))
        qts.append(qt)
        kts.append(kt)

    s0 = [[state_ref[si, h] for h in hr] for si in sr]
    reds = [[jnp.dot(xss[si][bi], wseg, preferred_element_type=F32) for bi in range(nb)] for si in sr]
    o_in = [[_dot_nt(qgs[si][:, ksl[h]], s0[si][h]) for h in hr] for si in sr]
    sc = [[[_dot_nt(qts[si][bi][:, ksl[h]], kts[si][bi][:, ksl[h]]).astype(BF16) for h in hr]
           for bi in range(1, nb)] for si in sr]
    off = [[[_dot(sc[si][bi - 1][h], vbs[si][:bi * sub, vsl[h]]) for h in hr]
            for bi in range(1, nb)] for si in sr]
    s_up = [[_dot_tn(vbs[si][:, vsl[h]], kds[si][:, ksl[h]]) for h in hr] for si in sr]

    for si in sr:
        blocks = []
        for bi in range(nb):
            r0 = bi * sub
            v_i = vs[si][r0:r0 + sub]
            red = reds[si][bi]
            o_i = red[0:sub] * v_i[0:1]
            for jj in range(1, sub):
                o_i = o_i + red[jj * sub:(jj + 1) * sub] * v_i[jj:jj + 1]
            if bi > 0:
                o_i = o_i + jnp.concatenate(off[si][bi - 1], axis=-1)
            blocks.append(o_i)
        o = (jnp.concatenate(blocks, axis=0) if nb > 1 else blocks[0]) + jnp.concatenate(o_in[si], axis=-1)
        for h in hr:
            state_ref[si, h] = s0[si][h] * egls[si][:, ksl[h]] + s_up[si][h]
            o_h = _rms(o[:, vsl[h]], ng_ref[...])
            g_h = gs[si][:, vsl[h]]
            y_ref[si, :, vsl[h]] = (o_h * (g_h * _sigmoid(g_h))).astype(y_ref.dtype)

    @pl.when(n == pl.num_programs(1) - 1)
    def _():
        s_ref[...] = state_ref[...]


def gla_mixer(p, s0, prm, heads, dk, dv, lora):
    b, t, cols = p.shape
    c = min(CHUNK, t)
    hk, hv = heads * dk, heads * dv
    wseg = (jnp.arange(hk)[:, None] // dk == jnp.arange(hv)[None, :] // dv).astype(BF16)
    consts = [prm['w_gk2'], prm['b_gk'].reshape(1, hk), prm['gla_norm_g'].reshape(1, dv), wseg]
    cspec = lambda a: pl.BlockSpec(a.shape, lambda i, j: (0, 0))
    bb = _seq_block(b, heads)
    y, s = pl.pallas_call(
        functools.partial(_gla_kernel, heads=heads, dk=dk, dv=dv, lora=lora),
        grid=(b // bb, t // c),
        in_specs=[pl.BlockSpec((bb, c, cols), lambda i, j: (i, j, 0)),
                  pl.BlockSpec((bb, heads, dv, dk), lambda i, j: (i, 0, 0, 0))] + [cspec(a) for a in consts],
        out_specs=[pl.BlockSpec((bb, c, hv), lambda i, j: (i, j, 0)),
                   pl.BlockSpec((bb, heads, dv, dk), lambda i, j: (i, 0, 0, 0))],
        out_shape=[jax.ShapeDtypeStruct((b, t, hv), BF16),
                   jax.ShapeDtypeStruct((b, heads, dv, dk), F32)],
        scratch_shapes=[pltpu.VMEM((bb, heads, dv, dk), F32)],
        compiler_params=_cparams("parallel", "arbitrary"),
    )(p, jnp.swapaxes(s0, -1, -2), *consts)
    return y, jnp.swapaxes(s, -1, -2)


def _gdn_kernel(p_ref, conv0_ref, s0_ref, cw_ref, alog_ref, dtb_ref, ng_ref,
                y_ref, conv_ref, s_ref,
                xp_ref, state_ref, *, heads, dk, dv, conv_w):
    n = pl.program_id(1)
    bb, c, _ = p_ref.shape
    cq = 2 * heads * dk + heads * dv
    pad = 8

    @pl.when(n == 0)
    def _():
        xp_ref[:, 0:pad, :] = conv0_ref[...]
        state_ref[...] = s0_ref[...]

    tril = _tri(c, False)
    strict = _tri(c, True)
    tril_bf = tril.astype(BF16)
    triu_bf = (lax.broadcasted_iota(jnp.int32, (c, c), 0) <= lax.broadcasted_iota(jnp.int32, (c, c), 1)).astype(BF16)

    ch = [(si, h) for si in range(bb) for h in range(heads)]
    cr = range(len(ch))
    qs, qgs, ks, kds, kbs, rhss, decs, egl_last = [], [], [], [], [], [], [], []
    for si in range(bb):
        x = p_ref[si, :, 0:cq]
        tail = xp_ref[si, 0:pad, :]
        row8 = lax.broadcasted_iota(jnp.int32, (pad, cq), 0)
        conv = x * cw_ref[conv_w - 1:conv_w, :]
        for s in range(1, conv_w):
            xs = pltpu.roll(x, shift=s, axis=0)
            head = jnp.where(row8 < s, pltpu.roll(tail, shift=s, axis=0), xs[0:pad])
            xs = jnp.concatenate([head, xs[pad:]], axis=0) if c > pad else head
            conv = conv + xs * cw_ref[conv_w - 1 - s:conv_w - s, :]
        xp_ref[si, 0:pad, :] = x[c - pad:c]
        qkv = conv * _sigmoid(conv)

        ab = p_ref[si, :, cq + heads * dv:]
        beta = _sigmoid(ab)
        g = -jnp.exp(alog_ref[...]) * _softplus(ab + dtb_ref[...])
        gc = _dot_exact_lhs(tril_bf, g)
        gct = None
        for part in _split(g, 3):
            d = lax.dot_general(part, triu_bf, (((0,), (0,)), ((), ())), preferred_element_type=F32)
            gct = d if gct is None else gct + d
        gl = gc[c - 1:c, :]
        eg = jnp.exp(gc)
        egl = jnp.exp(gl - gc)
        egl_last.append(jnp.exp(gl))
        for h in range(heads):
            q = qkv[:, h * dk:(h + 1) * dk]
            k = qkv[:, heads * dk + h * dk:heads * dk + (h + 1) * dk]
            v = qkv[:, 2 * heads * dk + h * dv:2 * heads * dk + (h + 1) * dv]
            q = q * lax.rsqrt(jnp.sum(q * q, axis=-1, keepdims=True) + EPS) * (dk ** -0.5)
            k = k * lax.rsqrt(jnp.sum(k * k, axis=-1, keepdims=True) + EPS)
            bcol = beta[:, heads + h:heads + h + 1]
            diff = gc[:, h:h + 1] - gct[h:h + 1, :]
            decs.append(jnp.where(tril, jnp.exp(jnp.where(tril, diff, 0.0)), 0.0))
            kb = k * bcol
            qs.append(q.astype(BF16))
            qgs.append((q * eg[:, h:h + 1]).astype(BF16))
            ks.append(k.astype(BF16))
            kds.append((k * egl[:, h:h + 1]).astype(BF16))
            kbs.append(kb.astype(BF16))
            rhss.append(jnp.concatenate([v * bcol, kb * eg[:, h:h + 1]], axis=-1))
    lmat = [jnp.where(strict, _dot_nt(kbs[i], ks[i]) * decs[i], 0.0) for i in cr]
    att = [(_dot_nt(qs[i], ks[i]) * decs[i]).astype(BF16) for i in cr]
    uw = _unit_lower_solve(lmat, rhss)
    s0 = [state_ref[si, h] for si, h in ch]
    s0b = [s.astype(BF16) for s in s0]
    v_new = [uw[i][:, :dv] - _dot(uw[i][:, dv:], s0b[i]) for i in cr]
    o_in = [_dot(qgs[i], s0b[i]) for i in cr]
    o_at = [_dot(att[i], v_new[i]) for i in cr]
    s_up = [_dot_tn(kds[i], v_new[i]) for i in cr]
    for i in cr:
        si, h = ch[i]
        state_ref[si, h] = s0[i] * egl_last[si][:, h:h + 1] + s_up[i]
        o = _rms(o_in[i] + o_at[i], ng_ref[...])
        z_h = p_ref[si, :, cq + h * dv:cq + (h + 1) * dv]
        y_ref[si, :, h * dv:(h + 1) * dv] = (o * (z_h * _sigmoid(z_h))).astype(y_ref.dtype)

    @pl.when(n == pl.num_programs(1) - 1)
    def _():
        conv_ref[...] = xp_ref[:, 0:pad, :]
        s_ref[...] = state_ref[...]


def gdn_mixer(p, conv0, s0, prm, heads, dk, dv, conv_w):
    b, t, cols = p.shape
    c = min(CHUNK, t)
    cq = 2 * heads * dk + heads * dv
    pad = 8
    lanep = lambda a: jnp.pad(a.reshape(1, -1), ((0, 0), (0, LANE - a.size)))
    conv0p = jnp.pad(conv0, ((0, 0), (pad - (conv_w - 1), 0), (0, 0)))
    cw = jnp.pad(prm['conv_w'], ((0, pad - conv_w), (0, 0)))
    consts = [cw, lanep(prm['a_log']), lanep(prm['dt_bias']), prm['gdn_norm_g'].reshape(1, dv)]
    cspec = lambda a: pl.BlockSpec(a.shape, lambda i, j: (0, 0))
    bb = _seq_block(b, heads)
    y, conv, s = pl.pallas_call(
        functools.partial(_gdn_kernel, heads=heads, dk=dk, dv=dv, conv_w=conv_w),
        grid=(b // bb, t // c),
        in_specs=[pl.BlockSpec((bb, c, cols), lambda i, j: (i, j, 0)),
                  pl.BlockSpec((bb, pad, cq), lambda i, j: (i, 0, 0)),
                  pl.BlockSpec((bb, heads, dk, dv), lambda i, j: (i, 0, 0, 0))] + [cspec(a) for a in consts],
        out_specs=[pl.BlockSpec((bb, c, heads * dv), lambda i, j: (i, j, 0)),
                   pl.BlockSpec((bb, pad, cq), lambda i, j: (i, 0, 0)),
                   pl.BlockSpec((bb, heads, dk, dv), lambda i, j: (i, 0, 0, 0))],
        out_shape=[jax.ShapeDtypeStruct((b, t, heads * dv), BF16),
                   jax.ShapeDtypeStruct((b, pad, cq), F32),
                   jax.ShapeDtypeStruct((b, heads, dk, dv), F32)],
        scratch_shapes=[pltpu.VMEM((bb, pad, cq), F32),
                        pltpu.VMEM((bb, heads, dk, dv), F32)],
        compiler_params=_cparams("parallel", "arbitrary"),
    )(p, conv0p, s0, *consts)
    return y, conv[:, pad - (conv_w - 1):, :], s


D_MODEL = 2048
DEPTH = 2
H_A, N_A = 12, 64
D_A = H_A * N_A
LORA_W, LORA_A, LORA_G = 64, 64, 128
H_B, DK_B, DV_B, LORA_GK = 4, 64, 128, 16
H_C, DK_C, DV_C, CONV_W = 6, 128, 128, 4
A_COLS = 3 * D_A + LORA_W + LORA_A + LORA_G
B_COLS = 2 * H_B * DK_B + 2 * H_B * DV_B + LORA_GK
C_QKV = 2 * H_C * DK_C + H_C * DV_C
C_COLS = C_QKV + H_C * DV_C + 2 * H_C
H_X = 4


def _row_tile(m, cap):
    t = min(m, cap)
    while m % t:
        t //= 2
    return t


def _trunk(x, mem_k, mem_v, st_shift, st_rwkv, st_gla, st_conv, st_gdn, lw, final_g):
    b, t, d = x.shape
    m = b * t
    xf = x.reshape(m, d)
    tm_s = _row_tile(m, 512)
    tq = _row_tile(t, 256)
    outs = [[] for _ in range(5)]
    for l in range(DEPTH):
        w = lw[l]
        proj = lambda wi: norm_matmul(xf, w['norm1_g'], wi, tm_s, wi.shape[1]).reshape(b, t, -1)
        ya, sh, sa = rwkv_mixer(proj(w['w_in_a']), st_shift[l], st_rwkv[l], w, H_A, N_A, LORA_W, LORA_A)
        yb, sb = gla_mixer(proj(w['w_in_b']), st_gla[l], w, H_B, DK_B, DV_B, LORA_GK)
        yc, cb, sc = gdn_mixer(proj(w['w_in_c']), st_conv[l], st_gdn[l], w, H_C, DK_C, DV_C, CONV_W)
        x2 = cross_attn(xf.reshape(b, t, d), ya, yb, yc, w['w_out_a'], w['w_out_b'], w['w_out_c'],
                        w['normx_g'], w['wq_x'], w['wo_x'], mem_k[l], mem_v[l], H_X, tq)
        xf = mlp(x2.reshape(m, d), w['norm3_g'], final_g, w['w_up'], w['w_down'], tm_s, 1024,
                 final_norm=(l == DEPTH - 1))
        for lst, val in zip(outs, (sh, sa, sb, cb, sc)):
            lst.append(val)
    return (xf.reshape(b, t, d),) + tuple(jnp.stack(o) for o in outs)


def kernel(x_prompt, x_sample, state_rwkv_shift, state_rwkv, state_gla, state_gdn_conv, state_gdn, cache_mem_k, cache_mem_v, mem_prompt, norm1_g, w_in, mu_a, w0_a, w_w2, a0_a, w_a2, w_g2, k_k, k_a, r_k, lnx_g, lnx_b, w_gk2, b_gk, gla_norm_g, conv_w, a_log, dt_bias, gdn_norm_g, w_out, normx_g, normm_g, wq_x, wk_x, wv_x, wo_x, norm3_g, w_up, w_down, final_g):
    bf = lambda a: a.astype(BF16)
    w_in_a, w_in_b, w_in_c = split_cols(w_in, A_COLS, B_COLS, 256)
    lw = []
    for l in range(DEPTH):
        lw.append(dict(
            norm1_g=norm1_g[l],
            w_in_a=w_in_a[l], w_in_b=w_in_b[l], w_in_c=w_in_c[l],
            mu_a=mu_a[l], w0_a=w0_a[l], w_w2=w_w2[l], a0_a=a0_a[l], w_a2=w_a2[l], w_g2=w_g2[l],
            k_k=k_k[l], k_a=k_a[l], r_k=r_k[l], lnx_g=lnx_g[l], lnx_b=lnx_b[l],
            w_gk2=w_gk2[l], b_gk=b_gk[l], gla_norm_g=gla_norm_g[l],
            conv_w=conv_w[l], a_log=a_log[l], dt_bias=dt_bias[l], gdn_norm_g=gdn_norm_g[l],
            w_out_a=bf(w_out[l, :D_A]), w_out_b=bf(w_out[l, D_A:D_A + H_B * DV_B]),
            w_out_c=bf(w_out[l, D_A + H_B * DV_B:]),
            normx_g=normx_g[l], wq_x=bf(wq_x[l]), wo_x=bf(wo_x[l]),
            norm3_g=norm3_g[l], w_up=bf(w_up[l]), w_down=bf(w_down[l])))

    b, n_mem, d = mem_prompt.shape
    memf = mem_prompt.reshape(b * n_mem, d)
    hd_x = d // H_X
    tmm = _row_tile(b * n_mem, 1024)
    p_mem_k = jnp.stack([norm_matmul(memf, normm_g[l], bf(wk_x[l]), tmm, 512).reshape(b, n_mem, H_X, hd_x)
                         for l in range(DEPTH)])
    p_mem_v = jnp.stack([norm_matmul(memf, normm_g[l], bf(wv_x[l]), tmm, 512).reshape(b, n_mem, H_X, hd_x)
                         for l in range(DEPTH)])

    zeros = lambda *s: jnp.zeros((DEPTH, b) + s, F32)
    flat_mem = lambda a: bf(a.reshape(a.shape[0], a.shape[1], n_mem, d))
    y_prompt, p_shift, p_rwkv, p_gla, p_conv, p_gdn = _trunk(
        x_prompt, flat_mem(p_mem_k), flat_mem(p_mem_v),
        zeros(A_COLS), zeros(H_A, N_A, N_A), zeros(H_B, DK_B, DV_B), zeros(CONV_W - 1, C_QKV),
        zeros(H_C, DK_C, DV_C), lw, final_g)
    y_sample, s_shift, s_rwkv, s_gla, s_conv, s_gdn = _trunk(
        x_sample, flat_mem(cache_mem_k), flat_mem(cache_mem_v),
        state_rwkv_shift, state_rwkv, state_gla, state_gdn_conv, state_gdn, lw, final_g)
    return (y_prompt, y_sample, p_shift, p_rwkv, p_gla, p_conv, p_gdn, p_mem_k, p_mem_v,
            s_shift, s_rwkv, s_gla, s_conv, s_gdn)
```
